```python
import math
import jax, jax.numpy as jnp
from jax import lax
import numpy as np

D_MODEL = 1024
BATCH = 4
SEQ = 8192
DEPTH = 1

DN_HEADS = 8
DN_HEAD_DIM = 128
DN_QK_DIM = DN_HEADS * DN_HEAD_DIM
DN_V_DIM = DN_HEADS * DN_HEAD_DIM
CHUNK = 64
CONV_K = 4
LRU_HEADS = 10
LRU_BLOCK = 128
LRU_WIDTH = LRU_HEADS * LRU_BLOCK
LRU_C = 8.0
D_FF = -(-8 * D_MODEL // (3 * 256)) * 256
DEEPNORM_ALPHA = (2.0 * DEPTH) ** 0.25
DEEPNORM_BETA = (8.0 * DEPTH) ** -0.25
LN_EPS = 1e-5
RMS_EPS = 1e-6
IN_SIZES = [DN_QK_DIM, DN_QK_DIM, DN_V_DIM, DN_V_DIM, DN_HEADS, DN_HEADS,
            LRU_WIDTH, LRU_WIDTH, D_MODEL, D_MODEL]
IN_COLS = sum(IN_SIZES)
IN_SPLIT_IDX = [int(s) for s in np.cumsum(IN_SIZES)[:-1]]

kernel_name = "hybrid_gdn_rglru_swiglu_deepnorm"


def layer_norm(x, g, b):
    xf = x.astype(jnp.float32)
    mu = jnp.mean(xf, axis=-1, keepdims=True)
    var = jnp.mean(jnp.square(xf - mu), axis=-1, keepdims=True)
    y = (xf - mu) * lax.rsqrt(var + LN_EPS) * g.astype(jnp.float32) + b.astype(jnp.float32)
    return y.astype(x.dtype)


def causal_depthwise_conv(x, w):
    K, C = w.shape
    return lax.conv_general_dilated(
        x, w[:, None, :].astype(x.dtype), window_strides=(1,), padding=[(K - 1, 0)],
        dimension_numbers=("NWC", "WIO", "NWC"), feature_group_count=C)


def l2_normalize(x):
    return x * lax.rsqrt(jnp.sum(jnp.square(x), axis=-1, keepdims=True) + RMS_EPS)


def gated_delta_rule_chunked(q, k, v, g, beta):
    B, T, H, Dk = q.shape
    Dv = v.shape[-1]
    N = T // CHUNK

    def chunks(t):
        return t.reshape(B, N, CHUNK, H, -1).transpose(1, 0, 3, 2, 4)

    q = chunks(q) * (Dk ** -0.5)
    k = chunks(k)
    v = chunks(v)
    g = chunks(g[..., None])[..., 0]
    beta = chunks(beta[..., None])[..., 0]
    G = jnp.cumsum(g, axis=-1)

    idx = jnp.arange(CHUNK)
    causal = idx[:, None] >= idx[None, :]
    strict = idx[:, None] > idx[None, :]
    decay = jnp.exp(jnp.where(causal, G[..., :, None] - G[..., None, :], -jnp.inf))

    k_beta = k * beta[..., None]
    A = jnp.where(strict, jnp.einsum("nbhid,nbhjd->nbhij", k_beta, k) * decay, 0.0)
    T_mat = A + jnp.eye(CHUNK, dtype=A.dtype)
    rhs = jnp.concatenate([v * beta[..., None], k_beta * jnp.exp(G)[..., None]], axis=-1)
    sol = lax.linalg.triangular_solve(T_mat, rhs, left_side=True, lower=True, unit_diagonal=True)
    u = sol[..., :Dv]
    w = sol[..., Dv:]
    qk_intra = jnp.einsum("nbhid,nbhjd->nbhij", q, k) * decay

    def step(S, inp):
        q_c, k_c, u_c, w_c, G_c, qk_c = inp
        v_new = u_c - jnp.einsum("bhcd,bhde->bhce", w_c, S)
        o_c = (jnp.einsum("bhcd,bhde->bhce", q_c * jnp.exp(G_c)[..., None], S)
               + jnp.einsum("bhij,bhje->bhie", qk_c, v_new))
        G_last = G_c[..., -1]
        k_dec = k_c * jnp.exp(G_last[..., None] - G_c)[..., None]
        S = S * jnp.exp(G_last)[..., None, None] + jnp.einsum("bhcd,bhce->bhde", k_dec, v_new)
        return S, o_c

    S0 = jnp.zeros((B, H, Dk, Dv), jnp.float32)
    _, o = lax.scan(step, S0, (q, k, u, w, G, qk_intra))
    return o.transpose(1, 0, 3, 2, 4).reshape(B, T, H, Dv)


def rg_lru(u, w_a, b_a, w_i, b_i, lam):
    B, T, W = u.shape
    uf = u.astype(jnp.float32)
    ub = uf.reshape(B, T, LRU_HEADS, LRU_BLOCK)
    r = jax.nn.sigmoid(jnp.einsum("bthi,hij->bthj", ub, w_a.astype(jnp.float32)).reshape(B, T, W) + b_a)
    i = jax.nn.sigmoid(jnp.einsum("bthi,hij->bthj", ub, w_i.astype(jnp.float32)).reshape(B, T, W) + b_i)
    log_a = -LRU_C * r * jax.nn.softplus(-lam.astype(jnp.float32))
    a = jnp.exp(log_a)
    mult = jnp.sqrt(-jnp.expm1(2.0 * log_a))
    mult = jnp.where(jnp.arange(T)[None, :, None] == 0, 1.0, mult)
    b_in = mult * (i * uf)

    def combine(e1, e2):
        return (e1[0] * e2[0], e2[0] * e1[1] + e2[1])

    _, h = lax.associative_scan(combine, (a, b_in), axis=1)
    return h


def hybrid_mixer(x, w_in, dn_conv_w, dn_A_log, dn_dt_bias, dn_norm_w,
                 lru_conv_w, lru_conv_b, lru_w_a, lru_b_a, lru_w_i, lru_b_i, lru_lambda,
                 w_branch_dn, w_branch_lru, b_merge_gate, w_out):
    B, T, _ = x.shape
    proj = x @ w_in
    q, k, v, z, b_raw, a_raw, lru_x, lru_g, gate_dn, gate_lru = jnp.split(proj, IN_SPLIT_IDX, axis=-1)

    qkv = jax.nn.silu(causal_depthwise_conv(jnp.concatenate([q, k, v], axis=-1), dn_conv_w))
    q, k, v = jnp.split(qkv.astype(jnp.float32), [DN_QK_DIM, 2 * DN_QK_DIM], axis=-1)
    q = l2_normalize(q.reshape(B, T, DN_HEADS, DN_HEAD_DIM))
    k = l2_normalize(k.reshape(B, T, DN_HEADS, DN_HEAD_DIM))
    v = v.reshape(B, T, DN_HEADS, DN_HEAD_DIM)
    beta = jax.nn.sigmoid(b_raw.astype(jnp.float32))
    g = -jnp.exp(dn_A_log.astype(jnp.float32)) * jax.nn.softplus(
        a_raw.astype(jnp.float32) + dn_dt_bias.astype(jnp.float32))
    o = gated_delta_rule_chunked(q, k, v, g, beta)
    o = o * lax.rsqrt(jnp.mean(jnp.square(o), axis=-1, keepdims=True) + RMS_EPS) * dn_norm_w.astype(jnp.float32)
    o = o * jax.nn.silu(z.astype(jnp.float32).reshape(B, T, DN_HEADS, DN_HEAD_DIM))
    y_dn = o.reshape(B, T, DN_V_DIM).astype(x.dtype) @ w_branch_dn

    u = causal_depthwise_conv(lru_x, lru_conv_w) + lru_conv_b
    h = rg_lru(u, lru_w_a, lru_b_a, lru_w_i, lru_b_i, lru_lambda)
    y_lru = (h * jax.nn.gelu(lru_g.astype(jnp.float32))).astype(x.dtype) @ w_branch_lru

    gates = jax.nn.sigmoid(jnp.concatenate([gate_dn, gate_lru], axis=-1) + b_merge_gate)
    g_dn, g_lru = jnp.split(gates, 2, axis=-1)
    merged = g_dn * y_dn + g_lru * y_lru
    return merged @ w_out


def swiglu(x, w_gate, w_up, w_down):
    return (jax.nn.silu(x @ w_gate) * (x @ w_up)) @ w_down


def setup_inputs(seed: int = 0) -> dict:
    key = jax.random.key(seed)
    keys = jax.random.split(key, 32)
    L = DEPTH

    def nrm(k, shape, scale):
        return jax.random.normal(k, shape, jnp.float32) * scale

    x = nrm(keys[0], (BATCH, SEQ, D_MODEL), 1.0)
    col_scale = jnp.concatenate([
        jnp.full((s,), DEEPNORM_BETA if i == 2 else 1.0, jnp.float32)
        for i, s in enumerate(IN_SIZES)])
    w_in = nrm(keys[1], (L, D_MODEL, IN_COLS), D_MODEL ** -0.5) * col_scale
    dn_conv_w = nrm(keys[2], (L, CONV_K, 2 * DN_QK_DIM + DN_V_DIM), CONV_K ** -0.5)
    dn_A_log = jnp.log(jax.random.uniform(keys[3], (L, DN_HEADS), jnp.float32, 1.0, 16.0))
    dt = jnp.exp(jax.random.uniform(keys[4], (L, DN_HEADS), jnp.float32, math.log(1e-3), math.log(1e-1)))
    dn_dt_bias = dt + jnp.log(-jnp.expm1(-dt))
    dn_norm_w = 1.0 + nrm(keys[5], (L, DN_HEAD_DIM), 0.02)
    lru_conv_w = nrm(keys[6], (L, CONV_K, LRU_WIDTH), CONV_K ** -0.5)
    lru_conv_b = nrm(keys[7], (L, LRU_WIDTH), 0.02)
    lru_w_a = nrm(keys[8], (L, LRU_HEADS, LRU_BLOCK, LRU_BLOCK), LRU_BLOCK ** -0.5)
    lru_b_a = nrm(keys[9], (L, LRU_WIDTH), 0.02)
    lru_w_i = nrm(keys[10], (L, LRU_HEADS, LRU_BLOCK, LRU_BLOCK), LRU_BLOCK ** -0.5)
    lru_b_i = nrm(keys[11], (L, LRU_WIDTH), 0.02)
    a0 = jax.random.uniform(keys[12], (L, LRU_WIDTH), jnp.float32, 0.9, 0.999) ** (1.0 / LRU_C)
    lru_lambda = jnp.log(a0) - jnp.log1p(-a0)
    w_branch_dn = nrm(keys[13], (L, DN_V_DIM, D_MODEL), DN_V_DIM ** -0.5 * DEEPNORM_BETA)
    w_branch_lru = nrm(keys[14], (L, LRU_WIDTH, D_MODEL), LRU_WIDTH ** -0.5 * DEEPNORM_BETA)
    b_merge_gate = nrm(keys[15], (L, 2 * D_MODEL), 0.02)
    w_out = nrm(keys[16], (L, D_MODEL, D_MODEL), D_MODEL ** -0.5 * DEEPNORM_BETA)
    ln1_g = 1.0 + nrm(keys[17], (L, D_MODEL), 0.02)
    ln1_b = nrm(keys[18], (L, D_MODEL), 0.02)
    w_ffn_gate = nrm(keys[19], (L, D_MODEL, D_FF), D_MODEL ** -0.5 * DEEPNORM_BETA)
    w_ffn_up = nrm(keys[20], (L, D_MODEL, D_FF), D_MODEL ** -0.5 * DEEPNORM_BETA)
    w_ffn_down = nrm(keys[21], (L, D_FF, D_MODEL), D_FF ** -0.5 * DEEPNORM_BETA)
    ln2_g = 1.0 + nrm(keys[22], (L, D_MODEL), 0.02)
    ln2_b = nrm(keys[23], (L, D_MODEL), 0.02)
    return {"x": x, "w_in": w_in, "dn_conv_w": dn_conv_w, "dn_A_log": dn_A_log,
            "dn_dt_bias": dn_dt_bias, "dn_norm_w": dn_norm_w, "lru_conv_w": lru_conv_w,
            "lru_conv_b": lru_conv_b, "lru_w_a": lru_w_a, "lru_b_a": lru_b_a,
            "lru_w_i": lru_w_i, "lru_b_i": lru_b_i, "lru_lambda": lru_lambda,
            "w_branch_dn": w_branch_dn, "w_branch_lru": w_branch_lru,
            "b_merge_gate": b_merge_gate, "w_out": w_out, "ln1_g": ln1_g, "ln1_b": ln1_b,
            "w_ffn_gate": w_ffn_gate, "w_ffn_up": w_ffn_up, "w_ffn_down": w_ffn_down,
            "ln2_g": ln2_g, "ln2_b": ln2_b}


def reference(x, w_in, dn_conv_w, dn_A_log, dn_dt_bias, dn_norm_w, lru_conv_w, lru_conv_b,
              lru_w_a, lru_b_a, lru_w_i, lru_b_i, lru_lambda, w_branch_dn, w_branch_lru,
              b_merge_gate, w_out, ln1_g, ln1_b, w_ffn_gate, w_ffn_up, w_ffn_down,
              ln2_g, ln2_b):
    h = x
    for l in range(DEPTH):
        mix = hybrid_mixer(h, w_in[l], dn_conv_w[l], dn_A_log[l], dn_dt_bias[l], dn_norm_w[l],
                           lru_conv_w[l], lru_conv_b[l], lru_w_a[l], lru_b_a[l], lru_w_i[l],
                           lru_b_i[l], lru_lambda[l], w_branch_dn[l], w_branch_lru[l],
                           b_merge_gate[l], w_out[l])
        h = layer_norm(DEEPNORM_ALPHA * h + mix, ln1_g[l], ln1_b[l])
        ff = swiglu(h, w_ffn_gate[l], w_ffn_up[l], w_ffn_down[l])
        h = layer_norm(DEEPNORM_ALPHA * h + ff, ln2_g[l], ln2_b[l])
    return h
```

```python
import functools

import jax
import jax.numpy as jnp
from jax import lax
from jax.experimental import pallas as pl
from jax.experimental.pallas import tpu as pltpu

D_MODEL = 1024
DN_HEADS = 8
DN_HEAD_DIM = 128
DN_DIM = DN_HEADS * DN_HEAD_DIM
CHUNK = 64
CONV_K = 4
LRU_HEADS = 10
LRU_BLOCK = 128
LRU_WIDTH = LRU_HEADS * LRU_BLOCK
LRU_C = 8.0
D_FF = 2816
DEPTH = 1
DEEPNORM_ALPHA = (2.0 * DEPTH) ** 0.25
LN_EPS = 1e-5
RMS_EPS = 1e-6

LANES = 128
SUBLANES = 8
HALO = SUBLANES
MIXER_TILE = 256
FFN_TILE = 512
VMEM_LIMIT_BYTES = 56 * 1024 * 1024

_F32 = jnp.float32
_BF16 = jnp.bfloat16


def _dot(a, b):
    return jnp.dot(a.astype(_BF16), b.astype(_BF16), preferred_element_type=_F32)


def _dot_nt(a, b):
    return lax.dot_general(a.astype(_BF16), b.astype(_BF16), (((1,), (1,)), ((), ())),
                           preferred_element_type=_F32)


def _dot_tn(a, b):
    return lax.dot_general(a.astype(_BF16), b.astype(_BF16), (((0,), (0,)), ((), ())),
                           preferred_element_type=_F32)


def _sigmoid(x):
    return 1.0 / (1.0 + jnp.exp(-x))


def _silu(x):
    return x * _sigmoid(x)


def _softplus(x):
    return jnp.maximum(x, 0.0) + jnp.log1p(jnp.exp(-jnp.abs(x)))


def _gelu_tanh(x):
    c = 0.7978845608028654
    return 0.5 * x * (1.0 + jnp.tanh(c * (x + 0.044715 * (x * x * x))))


def _layer_norm(x, g, b):
    mu = jnp.mean(x, axis=-1, keepdims=True)
    xc = x - mu
    var = jnp.mean(xc * xc, axis=-1, keepdims=True)
    return xc * lax.rsqrt(var + LN_EPS) * g + b


def _shift_rows(x, s):
    return pltpu.roll(x, s, 0)


def _unit_lower_inverse(a, row, col):
    n = a.shape[0]
    eye = (row == col).astype(_F32)
    x = eye - jnp.where((row == col + 1) & ((col & 1) == 0), a, 0.0)
    s = 2
    while s < n:
        sh = s.bit_length() - 1
        bi = row >> sh
        bj = col >> sh
        t = jnp.where((bi == bj + 1) & ((bi & 1) == 1), a, 0.0)
        xb = x.astype(_BF16)
        y = jnp.dot(xb, t.astype(_BF16), preferred_element_type=_F32)
        x = x - jnp.dot(y.astype(_BF16), xb, preferred_element_type=_F32)
        s *= 2
    return x


def _mixer_kernel(x_ref, w_qkv_ref, w_z_ref, w_bd_ref, w_lru_ref, w_mg_ref,
                  dn_conv_ref, a_log_ref, dt_bias_ref, dn_norm_ref,
                  lru_conv_w_ref, lru_conv_b_ref, w_ai_ref, b_a_ref, b_i_ref, lam_ref,
                  w_bdn_ref, w_blru_ref, b_mg_ref, w_out_ref, ln_g_ref, ln_b_ref,
                  o_ref,
                  qkv_s, lrux_s, act_s, bd_s, u_s, w_s, qk_s, dno_s, state_s, hl_s):
    tt = x_ref.shape[0]
    n_chunks = tt // CHUNK
    t_idx = pl.program_id(1)

    @pl.when(t_idx == 0)
    def _():
        qkv_s[0:HALO, :] = jnp.zeros((HALO, 3 * DN_DIM), _F32)
        lrux_s[0:HALO, :] = jnp.zeros((HALO, LRU_WIDTH), _F32)
        state_s[...] = jnp.zeros(state_s.shape, _F32)
        hl_s[...] = jnp.zeros(hl_s.shape, _F32)

    x = x_ref[...]
    xb = x.astype(_BF16)

    qkv_s[HALO:HALO + tt, :] = jnp.dot(xb, w_qkv_ref[...], preferred_element_type=_F32)
    lru_proj = jnp.dot(xb, w_lru_ref[...], preferred_element_type=_F32)
    lrux_s[HALO:HALO + tt, :] = lru_proj[:, :LRU_WIDTH]
    lru_gate = lru_proj[:, LRU_WIDTH:]

    def causal_conv(ext_ref, w_ref, cols):
        acc = ext_ref[HALO:HALO + tt, cols] * w_ref[CONV_K - 1:CONV_K, cols]
        for k in range(CONV_K - 1):
            off = HALO - (CONV_K - 1) + k
            acc = acc + ext_ref[off:off + tt, cols] * w_ref[k:k + 1, cols]
        return acc

    for j in range(3 * DN_HEADS):
        cols = slice(j * LANES, (j + 1) * LANES)
        a = _silu(causal_conv(qkv_s, dn_conv_ref, cols))
        if j < 2 * DN_HEADS:
            a = a * lax.rsqrt(jnp.sum(a * a, axis=-1, keepdims=True) + RMS_EPS)
        if j < DN_HEADS:
            a = a * (DN_HEAD_DIM ** -0.5)
        act_s[:, cols] = a
    qkv_s[0:HALO, :] = qkv_s[tt:tt + HALO, :]

    bd = jnp.dot(xb, w_bd_ref[...], preferred_element_type=_F32)
    beta_all = _sigmoid(bd)
    g_all = -jnp.exp(a_log_ref[...]) * _softplus(bd + dt_bias_ref[...])
    row_in_chunk = lax.broadcasted_iota(jnp.int32, (tt, LANES), 0) & (CHUNK - 1)
    s = 1
    while s < CHUNK:
        g_all = g_all + jnp.where(row_in_chunk >= s, _shift_rows(g_all, s), 0.0)
        s *= 2
    lane = lax.broadcasted_iota(jnp.int32, (tt, LANES), 1)
    bd_s[...] = jnp.where(lane < DN_HEADS, beta_all, g_all)

    row = lax.broadcasted_iota(jnp.int32, (CHUNK, CHUNK), 0)
    col = lax.broadcasted_iota(jnp.int32, (CHUNK, CHUNK), 1)

    def phase_a(c, carry):
        rows = pl.ds(pl.multiple_of(c * CHUNK, CHUNK), CHUNK)
        bdc = bd_s[rows, :]
        bdt = bdc.T
        for h in range(DN_HEADS):
            hc = slice(h * LANES, (h + 1) * LANES)
            q = act_s[rows, hc]
            k = act_s[rows, slice(DN_DIM + h * LANES, DN_DIM + (h + 1) * LANES)]
            v = act_s[rows, slice(2 * DN_DIM + h * LANES, 2 * DN_DIM + (h + 1) * LANES)]
            beta = bdc[:, h:h + 1]
            g_col = bdc[:, DN_HEADS + h:DN_HEADS + h + 1]
            g_row = bdt[DN_HEADS + h:DN_HEADS + h + 1, :]
            decay = jnp.where(row >= col, jnp.exp(jnp.minimum(g_col - g_row, 0.0)), 0.0)
            kb = k * beta
            a = jnp.where(row > col, _dot_nt(kb, k) * decay, 0.0)
            qk_s[h, rows, :] = _dot_nt(q, k) * decay
            t_inv = _unit_lower_inverse(a, row, col)
            rhs = jnp.concatenate([v * beta, kb * jnp.exp(g_col)], axis=1)
            uw = _dot(t_inv, rhs)
            u_s[rows, hc] = uw[:, :LANES]
            w_s[rows, hc] = uw[:, LANES:]
        return carry

    lax.fori_loop(0, n_chunks, phase_a, 0)

    def phase_b(c, carry):
        rows = pl.ds(pl.multiple_of(c * CHUNK, CHUNK), CHUNK)
        bdc = bd_s[rows, :]
        for h in range(DN_HEADS):
            hc = slice(h * LANES, (h + 1) * LANES)
            q = act_s[rows, hc]
            k = act_s[rows, slice(DN_DIM + h * LANES, DN_DIM + (h + 1) * LANES)]
            g_col = bdc[:, DN_HEADS + h:DN_HEADS + h + 1]
            g_last = g_col[CHUNK - 1:CHUNK, :]
            st = state_s[h]
            stb = st.astype(_BF16)
            v_new = u_s[rows, hc] - jnp.dot(w_s[rows, hc].astype(_BF16), stb,
                                             preferred_element_type=_F32)
            o = (jnp.dot((q * jnp.exp(g_col)).astype(_BF16), stb, preferred_element_type=_F32)
                 + _dot(qk_s[h, rows, :], v_new))
            dno_s[rows, hc] = o
            k_dec = k * jnp.exp(g_last - g_col)
            state_s[h] = st * jnp.exp(g_last) + _dot_tn(k_dec, v_new)
        return carry

    lax.fori_loop(0, n_chunks, phase_b, 0)

    z = jnp.dot(xb, w_z_ref[...], preferred_element_type=_F32)
    for h in range(DN_HEADS):
        hc = slice(h * LANES, (h + 1) * LANES)
        o = dno_s[:, hc]
        o = o * lax.rsqrt(jnp.mean(o * o, axis=-1, keepdims=True) + RMS_EPS) * dn_norm_ref[...]
        dno_s[:, hc] = o * _silu(z[:, hc])
    y_dn = jnp.dot(dno_s[...].astype(_BF16), w_bdn_ref[...], preferred_element_type=_F32)

    row_t = lax.broadcasted_iota(jnp.int32, (tt, LANES), 0)
    first_pos = (row_t == 0) & (t_idx == 0)
    neg_c_softplus = -LRU_C * _softplus(-lam_ref[...])
    ys = []
    for j in range(LRU_HEADS):
        cols = slice(j * LANES, (j + 1) * LANES)
        u = causal_conv(lrux_s, lru_conv_w_ref, cols) + lru_conv_b_ref[:, cols]
        ai = jnp.dot(u.astype(_BF16), w_ai_ref[j], preferred_element_type=_F32)
        r = _sigmoid(ai[:, :LANES] + b_a_ref[:, cols])
        i_gate = _sigmoid(ai[:, LANES:] + b_i_ref[:, cols])
        log_a = r * neg_c_softplus[:, cols]
        a = jnp.exp(log_a)
        mult = jnp.sqrt(1.0 - jnp.exp(2.0 * log_a))
        mult = jnp.where(first_pos, 1.0, mult)
        b = mult * (i_gate * u)
        s = 1
        while s < tt:
            keep = row_t >= s
            b = jnp.where(keep, a * _shift_rows(b, s) + b, b)
            a = jnp.where(keep, a * _shift_rows(a, s), a)
            s *= 2
        hcur = b + a * hl_s[0:1, cols]
        hl_s[0:1, cols] = hcur[tt - 1:tt, :]
        ys.append(hcur * _gelu_tanh(lru_gate[:, cols]))
    lrux_s[0:HALO, :] = lrux_s[tt:tt + HALO, :]
    y_lru = jnp.dot(jnp.concatenate(ys, axis=1).astype(_BF16), w_blru_ref[...],
                    preferred_element_type=_F32)

    gates = _sigmoid(jnp.dot(xb, w_mg_ref[...], preferred_element_type=_F32) + b_mg_ref[...])
    merged = gates[:, :D_MODEL] * y_dn + gates[:, D_MODEL:] * y_lru
    mix = jnp.dot(merged.astype(_BF16), w_out_ref[...], preferred_element_type=_F32)
    o_ref[...] = _layer_norm(DEEPNORM_ALPHA * x + mix, ln_g_ref[...], ln_b_ref[...])


def _ffn_kernel(h_ref, w_gate_ref, w_up_ref, w_down_ref, ln_g_ref, ln_b_ref, o_ref):
    h = h_ref[...]
    hb = h.astype(_BF16)
    gate = jnp.dot(hb, w_gate_ref[...], preferred_element_type=_F32)
    up = jnp.dot(hb, w_up_ref[...], preferred_element_type=_F32)
    ff = jnp.dot((_silu(gate) * up).astype(_BF16), w_down_ref[...], preferred_element_type=_F32)
    o_ref[...] = _layer_norm(DEEPNORM_ALPHA * h + ff, ln_g_ref[...], ln_b_ref[...])


def _resident(shape):
    zeros = (0,) * len(shape)
    return pl.BlockSpec(shape, lambda *_: zeros, pipeline_mode=pl.Buffered(1))


def _pad_lanes(v, offset):
    return jnp.zeros((1, LANES), _F32).at[0, offset:offset + v.shape[0]].set(v.astype(_F32))


def _mixer(x, w_in, dn_conv_w, dn_a_log, dn_dt_bias, dn_norm_w, lru_conv_w, lru_conv_b,
           lru_w_a, lru_b_a, lru_w_i, lru_b_i, lru_lambda, w_branch_dn, w_branch_lru,
           b_merge_gate, w_out, ln_g, ln_b):
    bsz, seq, _ = x.shape
    tt = min(MIXER_TILE, seq)
    assert seq % tt == 0 and tt % CHUNK == 0

    o_q, o_z = 0, 3 * DN_DIM
    o_b = o_z + DN_DIM
    o_lru = o_b + 2 * DN_HEADS
    o_mg = o_lru + 2 * LRU_WIDTH
    w_qkv = w_in[:, o_q:o_z].astype(_BF16)
    w_z = w_in[:, o_z:o_b].astype(_BF16)
    w_bd = jnp.pad(w_in[:, o_b:o_lru], ((0, 0), (0, LANES - 2 * DN_HEADS))).astype(_BF16)
    w_lru = w_in[:, o_lru:o_mg].astype(_BF16)
    w_mg = w_in[:, o_mg:].astype(_BF16)
    w_ai = jnp.concatenate([lru_w_a, lru_w_i], axis=-1).astype(_BF16)

    def row(v):
        return v.reshape(1, -1).astype(_F32)

    operands = [
        (x, pl.BlockSpec((None, tt, D_MODEL), lambda b, t: (b, t, 0))),
        (w_qkv, None), (w_z, None), (w_bd, None), (w_lru, None), (w_mg, None),
        (dn_conv_w.astype(_F32), None),
        (_pad_lanes(dn_a_log, DN_HEADS), None), (_pad_lanes(dn_dt_bias, DN_HEADS), None),
        (row(dn_norm_w), None),
        (lru_conv_w.astype(_F32), None), (row(lru_conv_b), None), (w_ai, None),
        (row(lru_b_a), None), (row(lru_b_i), None), (row(lru_lambda), None),
        (w_branch_dn.astype(_BF16), None), (w_branch_lru.astype(_BF16), None),
        (row(b_merge_gate), None), (w_out.astype(_BF16), None), (row(ln_g), None), (row(ln_b), None),
    ]
    args = [a for a, _ in operands]
    in_specs = [spec if spec is not None else _resident(a.shape) for a, spec in operands]

    scratch = [
        pltpu.VMEM((tt + HALO, 3 * DN_DIM), _F32),
        pltpu.VMEM((tt + HALO, LRU_WIDTH), _F32),
        pltpu.VMEM((tt, 3 * DN_DIM), _F32),
        pltpu.VMEM((tt, LANES), _F32),
        pltpu.VMEM((tt, DN_DIM), _F32),
        pltpu.VMEM((tt, DN_DIM), _F32),
        pltpu.VMEM((DN_HEADS, tt, CHUNK), _F32),
        pltpu.VMEM((tt, DN_DIM), _F32),
        pltpu.VMEM((DN_HEADS, DN_HEAD_DIM, DN_HEAD_DIM), _F32),
        pltpu.VMEM((SUBLANES, LRU_WIDTH), _F32),
    ]
    return pl.pallas_call(
        _mixer_kernel,
        grid=(bsz, seq // tt),
        in_specs=in_specs,
        out_specs=pl.BlockSpec((None, tt, D_MODEL), lambda b, t: (b, t, 0)),
        out_shape=jax.ShapeDtypeStruct((bsz, seq, D_MODEL), _F32),
        scratch_shapes=scratch,
        compiler_params=pltpu.CompilerParams(
            dimension_semantics=("arbitrary", "arbitrary"), vmem_limit_bytes=VMEM_LIMIT_BYTES),
        name="hybrid_mixer_ln",
    )(*args)


def _ffn(h, w_gate, w_up, w_down, ln_g, ln_b):
    n, _ = h.shape
    tm = min(FFN_TILE, n)
    assert n % tm == 0
    args = [h, w_gate.astype(_BF16), w_up.astype(_BF16), w_down.astype(_BF16),
            ln_g.reshape(1, -1).astype(_F32), ln_b.reshape(1, -1).astype(_F32)]
    in_specs = [pl.BlockSpec((tm, D_MODEL), lambda i: (i, 0))] + [_resident(a.shape) for a in args[1:]]
    return pl.pallas_call(
        _ffn_kernel,
        grid=(n // tm,),
        in_specs=in_specs,
        out_specs=pl.BlockSpec((tm, D_MODEL), lambda i: (i, 0)),
        out_shape=jax.ShapeDtypeStruct((n, D_MODEL), _F32),
        compiler_params=pltpu.CompilerParams(
            dimension_semantics=("arbitrary",), vmem_limit_bytes=VMEM_LIMIT_BYTES),
        name="swiglu_ffn_ln",
    )(*args)


def kernel(x, w_in, dn_conv_w, dn_A_log, dn_dt_bias, dn_norm_w, lru_conv_w, lru_conv_b, lru_w_a, lru_b_a, lru_w_i, lru_b_i, lru_lambda, w_branch_dn, w_branch_lru, b_merge_gate, w_out, ln1_g, ln1_b, w_ffn_gate, w_ffn_up, w_ffn_down, ln2_g, ln2_b):
    bsz, seq, d = x.shape
    h = x
    for l in range(DEPTH):
        h = _mixer(h, w_in[l], dn_conv_w[l], dn_A_log[l], dn_dt_bias[l], dn_norm_w[l],
                   lru_conv_w[l], lru_conv_b[l], lru_w_a[l], lru_b_a[l], lru_w_i[l], lru_b_i[l],
                   lru_lambda[l], w_branch_dn[l], w_branch_lru[l], b_merge_gate[l], w_out[l],
                   ln1_g[l], ln1_b[l])
        h = _ffn(h.reshape(bsz * seq, d), w_ffn_gate[l], w_ffn_up[l], w_ffn_down[l],
                 ln2_g[l], ln2_b[l]).reshape(bsz, seq, d)
    return h
```

```python
import functools

import jax
import jax.numpy as jnp
from jax import lax
from jax.experimental import pallas as pl
from jax.experimental.pallas import tpu as pltpu

D_MODEL = 1024
DN_HEADS = 8
DN_HEAD_DIM = 128
DN_DIM = DN_HEADS * DN_HEAD_DIM
CHUNK = 64
CONV_K = 4
LRU_HEADS = 10
LRU_BLOCK = 128
LRU_WIDTH = LRU_HEADS * LRU_BLOCK
LRU_C = 8.0
D_FF = 2816
DEPTH = 1
DEEPNORM_ALPHA = (2.0 * DEPTH) ** 0.25
LN_EPS = 1e-5
RMS_EPS = 1e-6

LANES = 128
SUBLANES = 8
HALO = SUBLANES
MIXER_TILE = 256
FFN_TILE = 512
VMEM_LIMIT_BYTES = 56 * 1024 * 1024

_F32 = jnp.float32
_BF16 = jnp.bfloat16


def _dot_nt(a, b):
    return lax.dot_general(a.astype(_BF16), b.astype(_BF16), (((1,), (1,)), ((), ())),
                           preferred_element_type=_F32)


def _sigmoid(x):
    return 1.0 / (1.0 + jnp.exp(-x))


def _silu(x):
    return x * _sigmoid(x)


def _softplus(x):
    return jnp.maximum(x, 0.0) + jnp.log1p(jnp.exp(-jnp.abs(x)))


def _gelu_tanh(x):
    c = 0.7978845608028654
    return 0.5 * x * (1.0 + jnp.tanh(c * (x + 0.044715 * (x * x * x))))


def _layer_norm(x, g, b):
    mu = jnp.mean(x, axis=-1, keepdims=True)
    xc = x - mu
    var = jnp.mean(xc * xc, axis=-1, keepdims=True)
    return xc * lax.rsqrt(var + LN_EPS) * g + b


def _shift_rows(x, s):
    return pltpu.roll(x, s, 0)


def _unit_lower_inverses(a_list, row, col):
    n = a_list[0].shape[0]
    eye = (row == col).astype(_F32)
    first = (row == col + 1) & ((col & 1) == 0)
    xs = [eye - jnp.where(first, a, 0.0) for a in a_list]
    s = 2
    while s < n:
        sh = s.bit_length() - 1
        bi = row >> sh
        bj = col >> sh
        join = (bi == bj + 1) & ((bi & 1) == 1)
        xbs = [x.astype(_BF16) for x in xs]
        ys = [jnp.dot(xb, jnp.where(join, a, 0.0).astype(_BF16), preferred_element_type=_F32)
              for xb, a in zip(xbs, a_list)]
        xs = [x - jnp.dot(y.astype(_BF16), xb, preferred_element_type=_F32)
              for x, y, xb in zip(xs, ys, xbs)]
        s *= 2
    return xs


def _mixer_kernel(x_ref, w_qkv_ref, w_z_ref, w_bd_ref, w_lru_ref, w_mg_ref,
                  dn_conv_ref, a_log_ref, dt_bias_ref, dn_norm_ref,
                  lru_conv_w_ref, lru_conv_b_ref, w_ai_ref, b_a_ref, b_i_ref, lam_ref,
                  w_bdn_ref, w_blru_ref, b_mg_ref, w_out_ref, ln_g_ref, ln_b_ref,
                  o_ref,
                  qkv_s, lrux_s, act_s, bd_s, u_s, wq_s, qk_s, kdt_s, dno_s, state_s, hl_s):
    tt = x_ref.shape[0]
    n_chunks = tt // CHUNK
    t_idx = pl.program_id(1)

    @pl.when(t_idx == 0)
    def _():
        qkv_s[0:HALO, :] = jnp.zeros((HALO, 3 * DN_DIM), _F32)
        lrux_s[0:HALO, :] = jnp.zeros((HALO, LRU_WIDTH), _F32)
        state_s[...] = jnp.zeros(state_s.shape, _F32)
        hl_s[...] = jnp.zeros(hl_s.shape, _F32)

    x = x_ref[...]
    xb = x.astype(_BF16)

    qkv_s[HALO:HALO + tt, :] = jnp.dot(xb, w_qkv_ref[...], preferred_element_type=_F32)
    lru_proj = jnp.dot(xb, w_lru_ref[...], preferred_element_type=_F32)
    lrux_s[HALO:HALO + tt, :] = lru_proj[:, :LRU_WIDTH]
    lru_gate = lru_proj[:, LRU_WIDTH:]

    def causal_conv(ext_ref, w_ref, cols):
        acc = ext_ref[HALO:HALO + tt, cols] * w_ref[CONV_K - 1:CONV_K, cols]
        for k in range(CONV_K - 1):
            off = HALO - (CONV_K - 1) + k
            acc = acc + ext_ref[off:off + tt, cols] * w_ref[k:k + 1, cols]
        return acc

    for j in range(3 * DN_HEADS):
        cols = slice(j * LANES, (j + 1) * LANES)
        a = _silu(causal_conv(qkv_s, dn_conv_ref, cols))
        if j < 2 * DN_HEADS:
            a = a * lax.rsqrt(jnp.sum(a * a, axis=-1, keepdims=True) + RMS_EPS)
        if j < DN_HEADS:
            a = a * (DN_HEAD_DIM ** -0.5)
        act_s[:, cols] = a
    qkv_s[0:HALO, :] = qkv_s[tt:tt + HALO, :]

    bd = jnp.dot(xb, w_bd_ref[...], preferred_element_type=_F32)
    beta_all = _sigmoid(bd)
    g_all = -jnp.exp(a_log_ref[...]) * _softplus(bd + dt_bias_ref[...])
    row_in_chunk = lax.broadcasted_iota(jnp.int32, (tt, LANES), 0) & (CHUNK - 1)
    s = 1
    while s < CHUNK:
        g_all = g_all + jnp.where(row_in_chunk >= s, _shift_rows(g_all, s), 0.0)
        s *= 2
    lane = lax.broadcasted_iota(jnp.int32, (tt, LANES), 1)
    bd_s[...] = jnp.where(lane < DN_HEADS, beta_all, g_all)

    row = lax.broadcasted_iota(jnp.int32, (CHUNK, CHUNK), 0)
    col = lax.broadcasted_iota(jnp.int32, (CHUNK, CHUNK), 1)

    a_mats, rhs_list = [], []
    for c in range(n_chunks):
        rows = slice(c * CHUNK, (c + 1) * CHUNK)
        bdc = bd_s[rows, :]
        bdt = bdc.T
        for h in range(DN_HEADS):
            i = c * DN_HEADS + h
            q = act_s[rows, h * LANES:(h + 1) * LANES]
            k = act_s[rows, DN_DIM + h * LANES:DN_DIM + (h + 1) * LANES]
            v = act_s[rows, 2 * DN_DIM + h * LANES:2 * DN_DIM + (h + 1) * LANES]
            beta = bdc[:, h:h + 1]
            g_col = bdc[:, DN_HEADS + h:DN_HEADS + h + 1]
            g_row = bdt[DN_HEADS + h:DN_HEADS + h + 1, :]
            g_last = g_col[CHUNK - 1:CHUNK, :]
            decay = jnp.where(row >= col, jnp.exp(jnp.minimum(g_col - g_row, 0.0)), 0.0)
            e_g = jnp.exp(g_col)
            kb = k * beta
            kbf = k.astype(_BF16)
            a_mats.append(jnp.where(row > col, _dot_nt(kb, kbf) * decay, 0.0))
            qk_s[i] = (_dot_nt(q, kbf) * decay).astype(_BF16)
            rhs_list.append(jnp.concatenate([v * beta, kb * e_g], axis=1).astype(_BF16))
            wq_s[i, CHUNK:2 * CHUNK, :] = (q * e_g).astype(_BF16)
            kdt_s[i] = (k * jnp.exp(g_last - g_col)).T.astype(_BF16)
    t_invs = _unit_lower_inverses(a_mats, row, col)
    for c in range(n_chunks):
        rows = slice(c * CHUNK, (c + 1) * CHUNK)
        for h in range(DN_HEADS):
            i = c * DN_HEADS + h
            uw = jnp.dot(t_invs[i].astype(_BF16), rhs_list[i], preferred_element_type=_F32)
            u_s[rows, h * LANES:(h + 1) * LANES] = uw[:, :LANES]
            wq_s[i, 0:CHUNK, :] = uw[:, LANES:].astype(_BF16)

    for c in range(n_chunks):
        rows = slice(c * CHUNK, (c + 1) * CHUNK)
        res = []
        for h in range(DN_HEADS):
            res.append(jnp.dot(wq_s[c * DN_HEADS + h], state_s[h].astype(_BF16),
                               preferred_element_type=_F32))
        for h in range(DN_HEADS):
            i = c * DN_HEADS + h
            hc = slice(h * LANES, (h + 1) * LANES)
            v_new = (u_s[rows, hc] - res[h][:CHUNK]).astype(_BF16)
            dno_s[rows, hc] = res[h][CHUNK:] + jnp.dot(qk_s[i], v_new, preferred_element_type=_F32)
            g_last = bd_s[(c + 1) * CHUNK - 1:(c + 1) * CHUNK, DN_HEADS + h:DN_HEADS + h + 1]
            state_s[h] = state_s[h] * jnp.exp(g_last) + jnp.dot(kdt_s[i], v_new,
                                                                 preferred_element_type=_F32)

    z = jnp.dot(xb, w_z_ref[...], preferred_element_type=_F32)
    for h in range(DN_HEADS):
        hc = slice(h * LANES, (h + 1) * LANES)
        o = dno_s[:, hc]
        o = o * lax.rsqrt(jnp.mean(o * o, axis=-1, keepdims=True) + RMS_EPS) * dn_norm_ref[...]
        dno_s[:, hc] = o * _silu(z[:, hc])
    y_dn = jnp.dot(dno_s[...].astype(_BF16), w_bdn_ref[...], preferred_element_type=_F32)

    row_t = lax.broadcasted_iota(jnp.int32, (tt, LANES), 0)
    first_pos = (row_t == 0) & (t_idx == 0)
    sub3 = lax.broadcasted_iota(jnp.int32, (tt // SUBLANES, SUBLANES, LANES), 1)
    neg_c_softplus = -LRU_C * _softplus(-lam_ref[...])
    ys = []
    for j in range(LRU_HEADS):
        cols = slice(j * LANES, (j + 1) * LANES)
        u = causal_conv(lrux_s, lru_conv_w_ref, cols) + lru_conv_b_ref[:, cols]
        ai = jnp.dot(u.astype(_BF16), w_ai_ref[j], preferred_element_type=_F32)
        r = _sigmoid(ai[:, :LANES] + b_a_ref[:, cols])
        i_gate = _sigmoid(ai[:, LANES:] + b_i_ref[:, cols])
        log_a = r * neg_c_softplus[:, cols]
        a = jnp.exp(log_a)
        mult = jnp.sqrt(1.0 - jnp.exp(2.0 * log_a))
        mult = jnp.where(first_pos, 1.0, mult)
        b = mult * (i_gate * u)
        a3 = a.reshape(tt // SUBLANES, SUBLANES, LANES)
        b3 = b.reshape(tt // SUBLANES, SUBLANES, LANES)
        s = 1
        while s < SUBLANES:
            keep = sub3 >= s
            b3 = b3 + jnp.where(keep, a3 * pltpu.roll(b3, s, 1), 0.0)
            a3 = a3 * jnp.where(keep, pltpu.roll(a3, s, 1), 1.0)
            s *= 2
        carry = hl_s[0:1, cols]
        groups = []
        for i in range(tt // SUBLANES):
            h_i = b3[i] + a3[i] * carry
            carry = h_i[SUBLANES - 1:SUBLANES, :]
            groups.append(h_i)
        hl_s[0:1, cols] = carry
        hcur = jnp.concatenate(groups, axis=0)
        ys.append(hcur * _gelu_tanh(lru_gate[:, cols]))
    lrux_s[0:HALO, :] = lrux_s[tt:tt + HALO, :]
    y_lru = jnp.dot(jnp.concatenate(ys, axis=1).astype(_BF16), w_blru_ref[...],
                    preferred_element_type=_F32)

    gates = _sigmoid(jnp.dot(xb, w_mg_ref[...], preferred_element_type=_F32) + b_mg_ref[...])
    merged = gates[:, :D_MODEL] * y_dn + gates[:, D_MODEL:] * y_lru
    mix = jnp.dot(merged.astype(_BF16), w_out_ref[...], preferred_element_type=_F32)
    o_ref[...] = _layer_norm(DEEPNORM_ALPHA * x + mix, ln_g_ref[...], ln_b_ref[...])


def _ffn_kernel(h_ref, w_gate_ref, w_up_ref, w_down_ref, ln_g_ref, ln_b_ref, o_ref):
    h = h_ref[...]
    hb = h.astype(_BF16)
    gate = jnp.dot(hb, w_gate_ref[...], preferred_element_type=_F32)
    up = jnp.dot(hb, w_up_ref[...], preferred_element_type=_F32)
    ff = jnp.dot((_silu(gate) * up).astype(_BF16), w_down_ref[...], preferred_element_type=_F32)
    o_ref[...] = _layer_norm(DEEPNORM_ALPHA * h + ff, ln_g_ref[...], ln_b_ref[...])


def _resident(shape):
    zeros = (0,) * len(shape)
    return pl.BlockSpec(shape, lambda *_: zeros, pipeline_mode=pl.Buffered(1))


def _pad_lanes(v, offset):
    return jnp.zeros((1, LANES), _F32).at[0, offset:offset + v.shape[0]].set(v.astype(_F32))


def _mixer(x, w_in, dn_conv_w, dn_a_log, dn_dt_bias, dn_norm_w, lru_conv_w, lru_conv_b,
           lru_w_a, lru_b_a, lru_w_i, lru_b_i, lru_lambda, w_branch_dn, w_branch_lru,
           b_merge_gate, w_out, ln_g, ln_b):
    bsz, seq, _ = x.shape
    tt = min(MIXER_TILE, seq)
    assert seq % tt == 0 and tt % CHUNK == 0

    o_q, o_z = 0, 3 * DN_DIM
    o_b = o_z + DN_DIM
    o_lru = o_b + 2 * DN_HEADS
    o_mg = o_lru + 2 * LRU_WIDTH
    w_qkv = w_in[:, o_q:o_z].astype(_BF16)
    w_z = w_in[:, o_z:o_b].astype(_BF16)
    w_bd = jnp.pad(w_in[:, o_b:o_lru], ((0, 0), (0, LANES - 2 * DN_HEADS))).astype(_BF16)
    w_lru = w_in[:, o_lru:o_mg].astype(_BF16)
    w_mg = w_in[:, o_mg:].astype(_BF16)
    w_ai = jnp.concatenate([lru_w_a, lru_w_i], axis=-1).astype(_BF16)

    def row(v):
        return v.reshape(1, -1).astype(_F32)

    operands = [
        (x, pl.BlockSpec((None, tt, D_MODEL), lambda b, t: (b, t, 0))),
        (w_qkv, None), (w_z, None), (w_bd, None), (w_lru, None), (w_mg, None),
        (dn_conv_w.astype(_F32), None),
        (_pad_lanes(dn_a_log, DN_HEADS), None), (_pad_lanes(dn_dt_bias, DN_HEADS), None),
        (row(dn_norm_w), None),
        (lru_conv_w.astype(_F32), None), (row(lru_conv_b), None), (w_ai, None),
        (row(lru_b_a), None), (row(lru_b_i), None), (row(lru_lambda), None),
        (w_branch_dn.astype(_BF16), None), (w_branch_lru.astype(_BF16), None),
        (row(b_merge_gate), None), (w_out.astype(_BF16), None), (row(ln_g), None), (row(ln_b), None),
    ]
    args = [a for a, _ in operands]
    in_specs = [spec if spec is not None else _resident(a.shape) for a, spec in operands]

    n_inst = (tt // CHUNK) * DN_HEADS
    scratch = [
        pltpu.VMEM((tt + HALO, 3 * DN_DIM), _F32),
        pltpu.VMEM((tt + HALO, LRU_WIDTH), _F32),
        pltpu.VMEM((tt, 3 * DN_DIM), _F32),
        pltpu.VMEM((tt, LANES), _F32),
        pltpu.VMEM((tt, DN_DIM), _F32),
        pltpu.VMEM((n_inst, 2 * CHUNK, DN_HEAD_DIM), _BF16),
        pltpu.VMEM((n_inst, CHUNK, CHUNK), _BF16),
        pltpu.VMEM((n_inst, DN_HEAD_DIM, CHUNK), _BF16),
        pltpu.VMEM((tt, DN_DIM), _F32),
        pltpu.VMEM((DN_HEADS, DN_HEAD_DIM, DN_HEAD_DIM), _F32),
        pltpu.VMEM((SUBLANES, LRU_WIDTH), _F32),
    ]
    return pl.pallas_call(
        _mixer_kernel,
        grid=(bsz, seq // tt),
        in_specs=in_specs,
        out_specs=pl.BlockSpec((None, tt, D_MODEL), lambda b, t: (b, t, 0)),
        out_shape=jax.ShapeDtypeStruct((bsz, seq, D_MODEL), _F32),
        scratch_shapes=scratch,
        compiler_params=pltpu.CompilerParams(
            dimension_semantics=("arbitrary", "arbitrary"), vmem_limit_bytes=VMEM_LIMIT_BYTES),
        name="hybrid_mixer_ln",
    )(*args)


def _ffn(h, w_gate, w_up, w_down, ln_g, ln_b):
    n, _ = h.shape
    tm = min(FFN_TILE, n)
    assert n % tm == 0
    args = [h, w_gate.astype(_BF16), w_up.astype(_BF16), w_down.astype(_BF16),
            ln_g.reshape(1, -1).astype(_F32), ln_b.reshape(1, -1).astype(_F32)]
    in_specs = [pl.BlockSpec((tm, D_MODEL), lambda i: (i, 0))] + [_resident(a.shape) for a in args[1:]]
    return pl.pallas_call(
        _ffn_kernel,
        grid=(n // tm,),
        in_specs=in_specs,
        out_specs=pl.BlockSpec((tm, D_MODEL), lambda i: (i, 0)),
        out_shape=jax.ShapeDtypeStruct((n, D_MODEL), _F32),
        compiler_params=pltpu.CompilerParams(
            dimension_semantics=("arbitrary",), vmem_limit_bytes=VMEM_LIMIT_BYTES),
        name="swiglu_ffn_ln",
    )(*args)


def kernel(x, w_in, dn_conv_w, dn_A_log, dn_dt_bias, dn_norm_w, lru_conv_w, lru_conv_b, lru_w_a, lru_b_a, lru_w_i, lru_b_i, lru_lambda, w_branch_dn, w_branch_lru, b_merge_gate, w_out, ln1_g, ln1_b, w_ffn_gate, w_ffn_up, w_ffn_down, ln2_g, ln2_b):
    bsz, seq, d = x.shape
    h = x
    for l in range(DEPTH):
        h = _mixer(h, w_in[l], dn_conv_w[l], dn_A_log[l], dn_dt_bias[l], dn_norm_w[l],
                   lru_conv_w[l], lru_conv_b[l], lru_w_a[l], lru_b_a[l], lru_w_i[l], lru_b_i[l],
                   lru_lambda[l], w_branch_dn[l], w_branch_lru[l], b_merge_gate[l], w_out[l],
                   ln1_g[l], ln1_b[l])
        h = _ffn(h.reshape(bsz * seq, d), w_ffn_gate[l], w_ffn_up[l], w_ffn_down[l],
                 ln2_g[l], ln2_b[l]).reshape(bsz, seq, d)
    return h
```

```python
import jax
import jax.numpy as jnp
from jax import lax
from jax.experimental import pallas as pl
from jax.experimental.pallas import tpu as pltpu

D_MODEL = 1024
DN_HEADS = 8
DN_HEAD_DIM = 128
DN_DIM = DN_HEADS * DN_HEAD_DIM
CHUNK = 64
CONV_K = 4
LRU_HEADS = 10
LRU_BLOCK = 128
LRU_WIDTH = LRU_HEADS * LRU_BLOCK
LRU_C = 8.0
DEPTH = 1
DEEPNORM_ALPHA = (2.0 * DEPTH) ** 0.25
LN_EPS = 1e-5
RMS_EPS = 1e-6

LANES = 128
SUBLANES = 8
MXU_COLS = 256
MIXER_TILE = 256
FFN_TILE = 512
VMEM_LIMIT_BYTES = 56 * 1024 * 1024

_F32 = jnp.float32
_BF16 = jnp.bfloat16


def _dot_nt(a, b):
    return lax.dot_general(a.astype(_BF16), b.astype(_BF16), (((1,), (1,)), ((), ())),
                           preferred_element_type=_F32)


def _sigmoid(x):
    return 0.5 * jnp.tanh(0.5 * x) + 0.5


def _silu(x):
    h = 0.5 * x
    return h * jnp.tanh(h) + h


def _softplus(x):
    return jnp.maximum(x, 0.0) + jnp.log1p(jnp.exp(-jnp.abs(x)))


def _gelu_tanh(x):
    c = 0.7978845608028654
    return 0.5 * x * (1.0 + jnp.tanh(c * (x + 0.044715 * (x * x * x))))


def _layer_norm(x, g, b):
    mu = jnp.mean(x, axis=-1, keepdims=True)
    xc = x - mu
    var = jnp.mean(xc * xc, axis=-1, keepdims=True)
    return xc * lax.rsqrt(var + LN_EPS) * g + b


def _unit_lower_inverses(a_list, row, col):
    n = a_list[0].shape[0]
    eye = (row == col).astype(_F32)
    first = (row == col + 1) & ((col & 1) == 0)
    xs = [eye - jnp.where(first, a, 0.0) for a in a_list]
    s = 2
    while s < n:
        sh = s.bit_length() - 1
        bi = row >> sh
        bj = col >> sh
        join = (bi == bj + 1) & ((bi & 1) == 1)
        xbs = [x.astype(_BF16) for x in xs]
        ys = [jnp.dot(xb, jnp.where(join, a, 0.0).astype(_BF16), preferred_element_type=_F32)
              for xb, a in zip(xbs, a_list)]
        xs = [x - jnp.dot(y.astype(_BF16), xb, preferred_element_type=_F32)
              for x, y, xb in zip(xs, ys, xbs)]
        s *= 2
    return xs


def _mixer_kernel(*refs):
    nb = SUBLANES
    x_refs = refs[:nb]
    (w_qkv_ref, w_z_ref, w_bd_ref, w_lru_ref, w_mg_ref,
     dn_conv_ref, a_log_ref, dt_bias_ref, dn_norm_ref,
     lru_conv_w_ref, lru_conv_b_ref, w_ai_ref, b_a_ref, b_i_ref, lam_ref,
     w_bdn_ref, w_blru_ref, b_mg_ref, w_out_ref, ln_g_ref, ln_b_ref,
     o_ref,
     qkv_s, lrux_s, qkv_tail_s, lrux_tail_s, lrug_s, gates_s, zs_s, ylru_s, act_s, bd_s, u_s,
     wq_s, qk_s, kdt_s, dno_s, state_s, hl_s) = refs[nb:]
    grp = x_refs[0].shape[0]
    tt = grp * nb
    cpb = CHUNK // nb
    n_chunks = tt // CHUNK
    t_idx = pl.program_id(1)

    def blk(j):
        return slice(grp * j, grp * (j + 1))

    def chunk_rows(c, j):
        return slice(grp * j + cpb * c, grp * j + cpb * (c + 1))

    def gather_chunk(ref, c, cols):
        return jnp.concatenate([ref[chunk_rows(c, j), cols] for j in range(nb)], axis=0)

    @pl.when(t_idx == 0)
    def _():
        qkv_tail_s[...] = jnp.zeros(qkv_tail_s.shape, _F32)
        lrux_tail_s[...] = jnp.zeros(lrux_tail_s.shape, _F32)
        state_s[...] = jnp.zeros(state_s.shape, _F32)
        hl_s[...] = jnp.zeros(hl_s.shape, _F32)

    x = jnp.concatenate([x_ref[...] for x_ref in x_refs], axis=0)
    xb = x.astype(_BF16)

    row_g = lax.broadcasted_iota(jnp.int32, (grp, LANES), 0)
    row_t = lax.broadcasted_iota(jnp.int32, (tt, LANES), 0)
    first_pos = (row_t == 0) & (t_idx == 0)
    sub3 = lax.broadcasted_iota(jnp.int32, (grp // SUBLANES, SUBLANES, LANES), 1)
    neg_c_softplus = -LRU_C * _softplus(-lam_ref[...])

    def causal_conv(raw_ref, tail_ref, w_ref, cols):
        blocks = [raw_ref[blk(j), cols] for j in range(nb)]
        wrapped = {j: jnp.where(row_g == 0, tail_ref[j:j + 1, cols], pltpu.roll(blocks[j], 1, 0))
                   for j in range(nb - (CONV_K - 1), nb)}
        outs = []
        for j in range(nb):
            acc = blocks[j] * w_ref[CONV_K - 1:CONV_K, cols]
            for k in range(CONV_K - 1):
                back = CONV_K - 1 - k
                src = blocks[j - back] if j >= back else wrapped[j - back + nb]
                acc = acc + src * w_ref[k:k + 1, cols]
            outs.append(acc)
        return jnp.concatenate(outs, axis=0)

    def save_tail(raw_ref, tail_ref):
        for j in range(nb - (CONV_K - 1), nb):
            tail_ref[j:j + 1, :] = raw_ref[grp * (j + 1) - 1:grp * (j + 1), :]

    def proj_tile(w_ref, i):
        cs = slice(i * MXU_COLS, (i + 1) * MXU_COLS)
        return cs, jnp.dot(xb, w_ref[:, cs], preferred_element_type=_F32)

    def qkv_task(i):
        cs, p = proj_tile(w_qkv_ref, i)
        qkv_s[:, cs] = p

    def lru_task(i):
        cs, p = proj_tile(w_lru_ref, i)
        n_x = LRU_WIDTH // MXU_COLS
        if i < n_x:
            lrux_s[:, cs] = p
        else:
            lrug_s[:, (i - n_x) * MXU_COLS:(i - n_x + 1) * MXU_COLS] = p

    def mg_task(i):
        cs, p = proj_tile(w_mg_ref, i)
        gates_s[:, cs] = _sigmoid(p + b_mg_ref[:, cs])

    def z_task(i):
        cs, p = proj_tile(w_z_ref, i)
        zs_s[:, cs] = _silu(p)

    def dn_conv_task(j):
        cols = slice(j * LANES, (j + 1) * LANES)
        a = _silu(causal_conv(qkv_s, qkv_tail_s, dn_conv_ref, cols))
        if j < 2 * DN_HEADS:
            a = a * lax.rsqrt(jnp.sum(a * a, axis=-1, keepdims=True) + RMS_EPS)
        if j < DN_HEADS:
            a = a * (DN_HEAD_DIM ** -0.5)
        act_s[:, cols] = a

    def lru_block_task(jh):
        cols = slice(jh * LANES, (jh + 1) * LANES)
        u = causal_conv(lrux_s, lrux_tail_s, lru_conv_w_ref, cols) + lru_conv_b_ref[:, cols]
        ai = jnp.dot(u.astype(_BF16), w_ai_ref[jh], preferred_element_type=_F32)
        r = _sigmoid(ai[:, :LANES] + b_a_ref[:, cols])
        i_gate = _sigmoid(ai[:, LANES:] + b_i_ref[:, cols])
        log_a = r * neg_c_softplus[:, cols]
        a = jnp.exp(log_a)
        mult = jnp.sqrt(1.0 - jnp.exp(2.0 * log_a))
        mult = jnp.where(first_pos, 1.0, mult)
        b = mult * (i_gate * u)
        h_loc, a_cum = [b[blk(0)]], [a[blk(0)]]
        for j in range(1, nb):
            h_loc.append(a[blk(j)] * h_loc[-1] + b[blk(j)])
            a_cum.append(a[blk(j)] * a_cum[-1])
        a3 = a_cum[nb - 1].reshape(grp // SUBLANES, SUBLANES, LANES)
        h3 = h_loc[nb - 1].reshape(grp // SUBLANES, SUBLANES, LANES)
        s = 1
        while s < SUBLANES:
            keep = sub3 >= s
            h3 = h3 + jnp.where(keep, a3 * pltpu.roll(h3, s, 1), 0.0)
            a3 = a3 * jnp.where(keep, pltpu.roll(a3, s, 1), 1.0)
            s *= 2
        h_prev = hl_s[0:1, cols]
        carry = h_prev
        row_end = []
        for i in range(grp // SUBLANES):
            e = h3[i] + a3[i] * carry
            carry = e[SUBLANES - 1:SUBLANES, :]
            row_end.append(e)
        hl_s[0:1, cols] = carry
        row_end = jnp.concatenate(row_end, axis=0)
        h_in = jnp.where(row_g == 0, h_prev, pltpu.roll(row_end, 1, 0))
        hcur = jnp.concatenate([h_loc[j] + a_cum[j] * h_in for j in range(nb)], axis=0)
        ylru_s[:, cols] = (hcur * _gelu_tanh(lrug_s[:, cols])).astype(_BF16)

    def tasks(fn, n):
        return [lambda i=i: fn(i) for i in range(n)]

    mxu_tasks = (tasks(qkv_task, 3 * DN_DIM // MXU_COLS) + tasks(lru_task, 2 * LRU_WIDTH // MXU_COLS)
                 + tasks(mg_task, 2 * D_MODEL // MXU_COLS) + tasks(z_task, DN_DIM // MXU_COLS))
    vpu_tasks = tasks(dn_conv_task, 3 * DN_HEADS) + tasks(lru_block_task, LRU_HEADS)
    mxu_tasks[0]()
    for i in range(max(len(mxu_tasks) - 1, len(vpu_tasks))):
        if i + 1 < len(mxu_tasks):
            mxu_tasks[i + 1]()
        if i < len(vpu_tasks):
            vpu_tasks[i]()
    save_tail(qkv_s, qkv_tail_s)
    save_tail(lrux_s, lrux_tail_s)

    bd = jnp.dot(xb, w_bd_ref[...], preferred_element_type=_F32)
    beta_all = _sigmoid(bd)
    g_all = -jnp.exp(a_log_ref[...]) * _softplus(bd + dt_bias_ref[...])
    prefix = [g_all[blk(0)]]
    for j in range(1, nb):
        prefix.append(prefix[-1] + g_all[blk(j)])
    row_total = prefix[nb - 1]
    row_in_chunk = row_g & (cpb - 1)
    incl = row_total
    s = 1
    while s < cpb:
        incl = incl + jnp.where(row_in_chunk >= s, pltpu.roll(incl, s, 0), 0.0)
        s *= 2
    before_row = incl - row_total
    g_cum = jnp.concatenate([p + before_row for p in prefix], axis=0)
    lane = lax.broadcasted_iota(jnp.int32, (tt, LANES), 1)
    bd_s[...] = jnp.where(lane < DN_HEADS, beta_all, g_cum)

    cpb_shift = cpb.bit_length() - 1
    nb_shift = nb.bit_length() - 1
    r_idx = lax.broadcasted_iota(jnp.int32, (CHUNK, CHUNK), 0)
    c_idx = lax.broadcasted_iota(jnp.int32, (CHUNK, CHUNK), 1)
    row = (r_idx >> cpb_shift) + ((r_idx & (cpb - 1)) << nb_shift)
    col = (c_idx >> cpb_shift) + ((c_idx & (cpb - 1)) << nb_shift)

    a_mats, rhs_list = [], []
    for c in range(n_chunks):
        bdc = gather_chunk(bd_s, c, slice(None))
        bdt = bdc.T
        for h in range(DN_HEADS):
            i = c * DN_HEADS + h
            q = gather_chunk(act_s, c, slice(h * LANES, (h + 1) * LANES))
            k = gather_chunk(act_s, c, slice(DN_DIM + h * LANES, DN_DIM + (h + 1) * LANES))
            v = gather_chunk(act_s, c, slice(2 * DN_DIM + h * LANES, 2 * DN_DIM + (h + 1) * LANES))
            beta = bdc[:, h:h + 1]
            g_col = bdc[:, DN_HEADS + h:DN_HEADS + h + 1]
            g_row = bdt[DN_HEADS + h:DN_HEADS + h + 1, :]
            g_last = g_col[CHUNK - 1:CHUNK, :]
            decay = jnp.where(row >= col, jnp.exp(jnp.minimum(g_col - g_row, 0.0)), 0.0)
            e_g = jnp.exp(g_col)
            kb = k * beta
            kbf = k.astype(_BF16)
            a_mats.append(jnp.where(row > col, _dot_nt(kb, kbf) * decay, 0.0))
            qk_s[i] = (_dot_nt(q, kbf) * decay).astype(_BF16)
            rhs_list.append(jnp.concatenate([v * beta, kb * e_g], axis=1).astype(_BF16))
            wq_s[i, CHUNK:2 * CHUNK, :] = (q * e_g).astype(_BF16)
            kdt_s[i] = (k * jnp.exp(g_last - g_col)).T.astype(_BF16)
    t_invs = _unit_lower_inverses(a_mats, row, col)
    for c in range(n_chunks):
        rows = slice(c * CHUNK, (c + 1) * CHUNK)
        for h in range(DN_HEADS):
            i = c * DN_HEADS + h
            uw = jnp.dot(t_invs[i].astype(_BF16), rhs_list[i], preferred_element_type=_F32)
            u_s[rows, h * LANES:(h + 1) * LANES] = uw[:, :LANES]
            wq_s[i, 0:CHUNK, :] = uw[:, LANES:].astype(_BF16)

    for c in range(n_chunks):
        rows = slice(c * CHUNK, (c + 1) * CHUNK)
        last_row = grp * (nb - 1) + cpb * (c + 1) - 1
        res = []
        for h in range(DN_HEADS):
            res.append(jnp.dot(wq_s[c * DN_HEADS + h], state_s[h].astype(_BF16),
                               preferred_element_type=_F32))
        for h in range(DN_HEADS):
            i = c * DN_HEADS + h
            hc = slice(h * LANES, (h + 1) * LANES)
            v_new = (u_s[rows, hc] - res[h][:CHUNK]).astype(_BF16)
            o = res[h][CHUNK:] + jnp.dot(qk_s[i], v_new, preferred_element_type=_F32)
            for j in range(nb):
                dno_s[chunk_rows(c, j), hc] = o[cpb * j:cpb * (j + 1)]
            g_last = bd_s[last_row:last_row + 1, DN_HEADS + h:DN_HEADS + h + 1]
            state_s[h] = state_s[h] * jnp.exp(g_last) + jnp.dot(kdt_s[i], v_new,
                                                                 preferred_element_type=_F32)

    for h in range(DN_HEADS):
        hc = slice(h * LANES, (h + 1) * LANES)
        o = dno_s[:, hc]
        o = o * lax.rsqrt(jnp.mean(o * o, axis=-1, keepdims=True) + RMS_EPS) * dn_norm_ref[...]
        dno_s[:, hc] = o * zs_s[:, hc]
    y_dn = jnp.dot(dno_s[...].astype(_BF16), w_bdn_ref[...], preferred_element_type=_F32)

    y_lru = jnp.dot(ylru_s[...], w_blru_ref[...], preferred_element_type=_F32)
    merged = gates_s[:, :D_MODEL] * y_dn + gates_s[:, D_MODEL:] * y_lru
    mix = jnp.dot(merged.astype(_BF16), w_out_ref[...], preferred_element_type=_F32)
    o_ref[...] = _layer_norm(DEEPNORM_ALPHA * x + mix, ln_g_ref[...], ln_b_ref[...])


def _ffn_kernel(h_ref, w_gate_ref, w_up_ref, w_down_ref, ln_g_ref, ln_b_ref, o_ref):
    tiles, rows, d = h_ref.shape
    h = h_ref[...].reshape(tiles * rows, d)
    hb = h.astype(_BF16)
    gate = jnp.dot(hb, w_gate_ref[...], preferred_element_type=_F32)
    up = jnp.dot(hb, w_up_ref[...], preferred_element_type=_F32)
    ff = jnp.dot((_silu(gate) * up).astype(_BF16), w_down_ref[...], preferred_element_type=_F32)
    out = _layer_norm(DEEPNORM_ALPHA * h + ff, ln_g_ref[...], ln_b_ref[...])
    o_ref[...] = out.reshape(tiles, rows, d)


def _resident(shape):
    zeros = (0,) * len(shape)
    return pl.BlockSpec(shape, lambda *_: zeros, pipeline_mode=pl.Buffered(1))


def _pad_lanes(v, offset):
    return jnp.zeros((1, LANES), _F32).at[0, offset:offset + v.shape[0]].set(v.astype(_F32))


def _mixer(x, w_in, dn_conv_w, dn_a_log, dn_dt_bias, dn_norm_w, lru_conv_w, lru_conv_b,
           lru_w_a, lru_b_a, lru_w_i, lru_b_i, lru_lambda, w_branch_dn, w_branch_lru,
           b_merge_gate, w_out, ln_g, ln_b):
    bsz, seq, _ = x.shape
    tt = min(MIXER_TILE, seq)
    assert seq % tt == 0 and tt % CHUNK == 0 and CHUNK == SUBLANES * SUBLANES

    o_q, o_z = 0, 3 * DN_DIM
    o_b = o_z + DN_DIM
    o_lru = o_b + 2 * DN_HEADS
    o_mg = o_lru + 2 * LRU_WIDTH
    w_qkv = w_in[:, o_q:o_z].astype(_BF16)
    w_z = w_in[:, o_z:o_b].astype(_BF16)
    w_bd = jnp.pad(w_in[:, o_b:o_lru], ((0, 0), (0, LANES - 2 * DN_HEADS))).astype(_BF16)
    w_lru = w_in[:, o_lru:o_mg].astype(_BF16)
    w_mg = w_in[:, o_mg:].astype(_BF16)
    w_ai = jnp.concatenate([lru_w_a, lru_w_i], axis=-1).astype(_BF16)

    def row(v):
        return v.reshape(1, -1).astype(_F32)

    grp = tt // SUBLANES
    x4 = x.reshape(bsz, seq // tt, grp, SUBLANES * D_MODEL)

    def x_block(j):
        return pl.BlockSpec((None, None, grp, D_MODEL), lambda b, t: (b, t, 0, j))

    operands = [(x4, x_block(j)) for j in range(SUBLANES)] + [
        (w_qkv, None), (w_z, None), (w_bd, None), (w_lru, None), (w_mg, None),
        (dn_conv_w.astype(_F32), None),
        (_pad_lanes(dn_a_log, DN_HEADS), None), (_pad_lanes(dn_dt_bias, DN_HEADS), None),
        (row(dn_norm_w), None),
        (lru_conv_w.astype(_F32), None), (row(lru_conv_b), None), (w_ai, None),
        (row(lru_b_a), None), (row(lru_b_i), None), (row(lru_lambda), None),
        (w_branch_dn.astype(_BF16), None), (w_branch_lru.astype(_BF16), None),
        (row(b_merge_gate), None), (w_out.astype(_BF16), None), (row(ln_g), None), (row(ln_b), None),
    ]
    args = [a for a, _ in operands]
    in_specs = [spec if spec is not None else _resident(a.shape) for a, spec in operands]

    n_inst = (tt // CHUNK) * DN_HEADS
    scratch = [
        pltpu.VMEM((tt, 3 * DN_DIM), _F32),
        pltpu.VMEM((tt, LRU_WIDTH), _F32),
        pltpu.VMEM((SUBLANES, 3 * DN_DIM), _F32),
        pltpu.VMEM((SUBLANES, LRU_WIDTH), _F32),
        pltpu.VMEM((tt, LRU_WIDTH), _F32),
        pltpu.VMEM((tt, 2 * D_MODEL), _F32),
        pltpu.VMEM((tt, DN_DIM), _F32),
        pltpu.VMEM((tt, LRU_WIDTH), _BF16),
        pltpu.VMEM((tt, 3 * DN_DIM), _F32),
        pltpu.VMEM((tt, LANES), _F32),
        pltpu.VMEM((tt, DN_DIM), _F32),
        pltpu.VMEM((n_inst, 2 * CHUNK, DN_HEAD_DIM), _BF16),
        pltpu.VMEM((n_inst, CHUNK, CHUNK), _BF16),
        pltpu.VMEM((n_inst, DN_HEAD_DIM, CHUNK), _BF16),
        pltpu.VMEM((tt, DN_DIM), _F32),
        pltpu.VMEM((DN_HEADS, DN_HEAD_DIM, DN_HEAD_DIM), _F32),
        pltpu.VMEM((SUBLANES, LRU_WIDTH), _F32),
    ]
    return pl.pallas_call(
        _mixer_kernel,
        grid=(bsz, seq // tt),
        in_specs=in_specs,
        out_specs=pl.BlockSpec((None, tt, D_MODEL), lambda b, t: (b, t, 0)),
        out_shape=jax.ShapeDtypeStruct((bsz, seq, D_MODEL), _F32),
        scratch_shapes=scratch,
        compiler_params=pltpu.CompilerParams(
            dimension_semantics=("arbitrary", "arbitrary"), vmem_limit_bytes=VMEM_LIMIT_BYTES),
        name="hybrid_mixer_ln",
    )(*args)


def _ffn(h, mixer_tile, w_gate, w_up, w_down, ln_g, ln_b):
    bsz, seq, _ = h.shape
    grp = mixer_tile // SUBLANES
    n_tiles = bsz * seq // mixer_tile
    tps = min(FFN_TILE // grp, n_tiles)
    assert n_tiles % tps == 0
    args = [h.reshape(n_tiles, SUBLANES, grp, D_MODEL),
            w_gate.astype(_BF16), w_up.astype(_BF16), w_down.astype(_BF16),
            ln_g.reshape(1, -1).astype(_F32), ln_b.reshape(1, -1).astype(_F32)]
    in_specs = ([pl.BlockSpec((tps, None, grp, D_MODEL), lambda i, j: (i, j, 0, 0))]
                + [_resident(a.shape) for a in args[1:]])
    out = pl.pallas_call(
        _ffn_kernel,
        grid=(n_tiles // tps, SUBLANES),
        in_specs=in_specs,
        out_specs=pl.BlockSpec((tps, grp, D_MODEL), lambda i, j: (i, 0, j)),
        out_shape=jax.ShapeDtypeStruct((n_tiles, grp, SUBLANES * D_MODEL), _F32),
        compiler_params=pltpu.CompilerParams(
            dimension_semantics=("arbitrary", "arbitrary"), vmem_limit_bytes=VMEM_LIMIT_BYTES),
        name="swiglu_ffn_ln",
    )(*args)
    return out.reshape(bsz, seq, D_MODEL)


def kernel(x, w_in, dn_conv_w, dn_A_log, dn_dt_bias, dn_norm_w, lru_conv_w, lru_conv_b, lru_w_a, lru_b_a, lru_w_i, lru_b_i, lru_lambda, w_branch_dn, w_branch_lru, b_merge_gate, w_out, ln1_g, ln1_b, w_ffn_gate, w_ffn_up, w_ffn_down, ln2_g, ln2_b):
    bsz, seq, d = x.shape
    h = x
    for l in range(DEPTH):
        h = _mixer(h, w_in[l], dn_conv_w[l], dn_A_log[l], dn_dt_bias[l], dn_norm_w[l],
                   lru_conv_w[l], lru_conv_b[l], lru_w_a[l], lru_b_a[l], lru_w_i[l], lru_b_i[l],
                   lru_lambda[l], w_branch_dn[l], w_branch_lru[l], b_merge_gate[l], w_out[l],
                   ln1_g[l], ln1_b[l])
        h = _ffn(h, min(MIXER_TILE, seq), w_ffn_gate[l], w_ffn_up[l], w_ffn_down[l],
                 ln2_g[l], ln2_b[l])
    return h
```

```python
import jax
import jax.numpy as jnp
from jax import lax
from jax.experimental import pallas as pl
from jax.experimental.pallas import tpu as pltpu

D_MODEL = 1024
DN_HEADS = 8
DN_HEAD_DIM = 128
DN_DIM = DN_HEADS * DN_HEAD_DIM
CHUNK = 64
CONV_K = 4
LRU_HEADS = 10
LRU_BLOCK = 128
LRU_WIDTH = LRU_HEADS * LRU_BLOCK
LRU_C = 8.0
DEPTH = 1
DEEPNORM_ALPHA = (2.0 * DEPTH) ** 0.25
LN_EPS = 1e-5
RMS_EPS = 1e-6

LANES = 128
SUBLANES = 8
MXU_COLS = 256
MIXER_TILE = 256
FFN_TILE = 512
VMEM_LIMIT_BYTES = 56 * 1024 * 1024

_F32 = jnp.float32
_BF16 = jnp.bfloat16


def _dot_nt(a, b):
    return lax.dot_general(a.astype(_BF16), b.astype(_BF16), (((1,), (1,)), ((), ())),
                           preferred_element_type=_F32)


def _sigmoid(x):
    return 0.5 * jnp.tanh(0.5 * x) + 0.5


def _silu(x):
    h = 0.5 * x
    return h * jnp.tanh(h) + h


def _softplus(x):
    return jnp.maximum(x, 0.0) + jnp.log1p(jnp.exp(-jnp.abs(x)))


def _gelu_tanh(x):
    c = 0.7978845608028654
    return 0.5 * x * (1.0 + jnp.tanh(c * (x + 0.044715 * (x * x * x))))


def _layer_norm(x, g, b):
    mu = jnp.mean(x, axis=-1, keepdims=True)
    xc = x - mu
    var = jnp.mean(xc * xc, axis=-1, keepdims=True)
    return xc * lax.rsqrt(var + LN_EPS) * g + b


def _unit_lower_inverses(a_list, row, col):
    n = a_list[0].shape[0]
    eye = (row == col).astype(_F32)
    first = (row == col + 1) & ((col & 1) == 0)
    xs = [eye - jnp.where(first, a, 0.0) for a in a_list]
    s = 2
    while s < n:
        sh = s.bit_length() - 1
        bi = row >> sh
        bj = col >> sh
        join = (bi == bj + 1) & ((bi & 1) == 1)
        xbs = [x.astype(_BF16) for x in xs]
        ys = [jnp.dot(xb, jnp.where(join, a, 0.0).astype(_BF16), preferred_element_type=_F32)
              for xb, a in zip(xbs, a_list)]
        xs = [x - jnp.dot(y.astype(_BF16), xb, preferred_element_type=_F32)
              for x, y, xb in zip(xs, ys, xbs)]
        s *= 2
    return xs


def _mixer_kernel(*refs):
    nb = SUBLANES
    n_xcols = D_MODEL // LANES
    x_refs = refs[:n_xcols]
    (w_qkv_ref, w_z_ref, w_bd_ref, w_lru_ref, w_mg_ref,
     dn_conv_ref, a_log_ref, dt_bias_ref, dn_norm_ref,
     lru_conv_w_ref, lru_conv_b_ref, w_ai_ref, b_a_ref, b_i_ref, lam_ref,
     w_bdn_ref, w_blru_ref, b_mg_ref, w_out_ref, ln_g_ref, ln_b_ref,
     o_ref,
     qkv_s, lrux_s, qkv_tail_s, lrux_tail_s, lrug_s, gates_s, zs_s, ylru_s, act_s, bd_s, u_s,
     wq_s, qk_s, kdt_s, dno_s, state_s, hl_s, out_s) = refs[n_xcols:]
    tt = x_refs[0].shape[0]
    grp = tt // nb
    cpb = CHUNK // nb
    n_chunks = tt // CHUNK
    t_idx = pl.program_id(1)

    def blk(j):
        return slice(grp * j, grp * (j + 1))

    def chunk_rows(c, j):
        return slice(grp * j + cpb * c, grp * j + cpb * (c + 1))

    def gather_chunk(ref, c, cols):
        return jnp.concatenate([ref[chunk_rows(c, j), cols] for j in range(nb)], axis=0)

    @pl.when(t_idx == 0)
    def _():
        qkv_tail_s[...] = jnp.zeros(qkv_tail_s.shape, _F32)
        lrux_tail_s[...] = jnp.zeros(lrux_tail_s.shape, _F32)
        state_s[...] = jnp.zeros(state_s.shape, _F32)
        hl_s[...] = jnp.zeros(hl_s.shape, _F32)

    x = jnp.concatenate(
        [jnp.concatenate([x_ref[pl.ds(j, grp, stride=nb), :] for x_ref in x_refs], axis=1)
         for j in range(nb)], axis=0)
    xb = x.astype(_BF16)

    row_g = lax.broadcasted_iota(jnp.int32, (grp, LANES), 0)
    row_t = lax.broadcasted_iota(jnp.int32, (tt, LANES), 0)
    first_pos = (row_t == 0) & (t_idx == 0)
    sub3 = lax.broadcasted_iota(jnp.int32, (grp // SUBLANES, SUBLANES, LANES), 1)
    neg_c_softplus = -LRU_C * _softplus(-lam_ref[...])

    def causal_conv(raw_ref, tail_ref, w_ref, cols):
        blocks = [raw_ref[blk(j), cols] for j in range(nb)]
        wrapped = {j: jnp.where(row_g == 0, tail_ref[j:j + 1, cols], pltpu.roll(blocks[j], 1, 0))
                   for j in range(nb - (CONV_K - 1), nb)}
        outs = []
        for j in range(nb):
            acc = blocks[j] * w_ref[CONV_K - 1:CONV_K, cols]
            for k in range(CONV_K - 1):
                back = CONV_K - 1 - k
                src = blocks[j - back] if j >= back else wrapped[j - back + nb]
                acc = acc + src * w_ref[k:k + 1, cols]
            outs.append(acc)
        return jnp.concatenate(outs, axis=0)

    def save_tail(raw_ref, tail_ref):
        for j in range(nb - (CONV_K - 1), nb):
            tail_ref[j:j + 1, :] = raw_ref[grp * (j + 1) - 1:grp * (j + 1), :]

    def proj_tile(w_ref, i):
        cs = slice(i * MXU_COLS, (i + 1) * MXU_COLS)
        return cs, jnp.dot(xb, w_ref[:, cs], preferred_element_type=_F32)

    def qkv_task(i):
        cs, p = proj_tile(w_qkv_ref, i)
        qkv_s[:, cs] = p

    def lru_task(i):
        cs, p = proj_tile(w_lru_ref, i)
        n_x = LRU_WIDTH // MXU_COLS
        if i < n_x:
            lrux_s[:, cs] = p
        else:
            lrug_s[:, (i - n_x) * MXU_COLS:(i - n_x + 1) * MXU_COLS] = p

    def mg_task(i):
        cs, p = proj_tile(w_mg_ref, i)
        gates_s[:, cs] = _sigmoid(p + b_mg_ref[:, cs])

    def z_task(i):
        cs, p = proj_tile(w_z_ref, i)
        zs_s[:, cs] = _silu(p)

    def dn_conv_task(j):
        cols = slice(j * LANES, (j + 1) * LANES)
        a = _silu(causal_conv(qkv_s, qkv_tail_s, dn_conv_ref, cols))
        if j < 2 * DN_HEADS:
            a = a * lax.rsqrt(jnp.sum(a * a, axis=-1, keepdims=True) + RMS_EPS)
        if j < DN_HEADS:
            a = a * (DN_HEAD_DIM ** -0.5)
        act_s[:, cols] = a

    def lru_block_task(jh):
        cols = slice(jh * LANES, (jh + 1) * LANES)
        u = causal_conv(lrux_s, lrux_tail_s, lru_conv_w_ref, cols) + lru_conv_b_ref[:, cols]
        ai = jnp.dot(u.astype(_BF16), w_ai_ref[jh], preferred_element_type=_F32)
        r = _sigmoid(ai[:, :LANES] + b_a_ref[:, cols])
        i_gate = _sigmoid(ai[:, LANES:] + b_i_ref[:, cols])
        log_a = r * neg_c_softplus[:, cols]
        a = jnp.exp(log_a)
        mult = jnp.sqrt(1.0 - jnp.exp(2.0 * log_a))
        mult = jnp.where(first_pos, 1.0, mult)
        b = mult * (i_gate * u)
        h_loc, a_cum = [b[blk(0)]], [a[blk(0)]]
        for j in range(1, nb):
            h_loc.append(a[blk(j)] * h_loc[-1] + b[blk(j)])
            a_cum.append(a[blk(j)] * a_cum[-1])
        a3 = a_cum[nb - 1].reshape(grp // SUBLANES, SUBLANES, LANES)
        h3 = h_loc[nb - 1].reshape(grp // SUBLANES, SUBLANES, LANES)
        s = 1
        while s < SUBLANES:
            keep = sub3 >= s
            h3 = h3 + jnp.where(keep, a3 * pltpu.roll(h3, s, 1), 0.0)
            a3 = a3 * jnp.where(keep, pltpu.roll(a3, s, 1), 1.0)
            s *= 2
        h_prev = hl_s[0:1, cols]
        carry = h_prev
        row_end = []
        for i in range(grp // SUBLANES):
            e = h3[i] + a3[i] * carry
            carry = e[SUBLANES - 1:SUBLANES, :]
            row_end.append(e)
        hl_s[0:1, cols] = carry
        row_end = jnp.concatenate(row_end, axis=0)
        h_in = jnp.where(row_g == 0, h_prev, pltpu.roll(row_end, 1, 0))
        hcur = jnp.concatenate([h_loc[j] + a_cum[j] * h_in for j in range(nb)], axis=0)
        ylru_s[:, cols] = (hcur * _gelu_tanh(lrug_s[:, cols])).astype(_BF16)

    def tasks(fn, n):
        return [lambda i=i: fn(i) for i in range(n)]

    mxu_tasks = (tasks(qkv_task, 3 * DN_DIM // MXU_COLS) + tasks(lru_task, 2 * LRU_WIDTH // MXU_COLS)
                 + tasks(mg_task, 2 * D_MODEL // MXU_COLS) + tasks(z_task, DN_DIM // MXU_COLS))
    vpu_tasks = tasks(dn_conv_task, 3 * DN_HEADS) + tasks(lru_block_task, LRU_HEADS)
    mxu_tasks[0]()
    for i in range(max(len(mxu_tasks) - 1, len(vpu_tasks))):
        if i + 1 < len(mxu_tasks):
            mxu_tasks[i + 1]()
        if i < len(vpu_tasks):
            vpu_tasks[i]()
    save_tail(qkv_s, qkv_tail_s)
    save_tail(lrux_s, lrux_tail_s)

    bd = jnp.dot(xb, w_bd_ref[...], preferred_element_type=_F32)
    beta_all = _sigmoid(bd)
    g_all = -jnp.exp(a_log_ref[...]) * _softplus(bd + dt_bias_ref[...])
    prefix = [g_all[blk(0)]]
    for j in range(1, nb):
        prefix.append(prefix[-1] + g_all[blk(j)])
    row_total = prefix[nb - 1]
    row_in_chunk = row_g & (cpb - 1)
    incl = row_total
    s = 1
    while s < cpb:
        incl = incl + jnp.where(row_in_chunk >= s, pltpu.roll(incl, s, 0), 0.0)
        s *= 2
    before_row = incl - row_total
    g_cum = jnp.concatenate([p + before_row for p in prefix], axis=0)
    lane = lax.broadcasted_iota(jnp.int32, (tt, LANES), 1)
    bd_s[...] = jnp.where(lane < DN_HEADS, beta_all, g_cum)

    cpb_shift = cpb.bit_length() - 1
    nb_shift = nb.bit_length() - 1
    r_idx = lax.broadcasted_iota(jnp.int32, (CHUNK, CHUNK), 0)
    c_idx = lax.broadcasted_iota(jnp.int32, (CHUNK, CHUNK), 1)
    row = (r_idx >> cpb_shift) + ((r_idx & (cpb - 1)) << nb_shift)
    col = (c_idx >> cpb_shift) + ((c_idx & (cpb - 1)) << nb_shift)

    a_mats, rhs_list = [], []
    for c in range(n_chunks):
        bdc = gather_chunk(bd_s, c, slice(None))
        bdt = bdc.T
        for h in range(DN_HEADS):
            i = c * DN_HEADS + h
            q = gather_chunk(act_s, c, slice(h * LANES, (h + 1) * LANES))
            k = gather_chunk(act_s, c, slice(DN_DIM + h * LANES, DN_DIM + (h + 1) * LANES))
            v = gather_chunk(act_s, c, slice(2 * DN_DIM + h * LANES, 2 * DN_DIM + (h + 1) * LANES))
            beta = bdc[:, h:h + 1]
            g_col = bdc[:, DN_HEADS + h:DN_HEADS + h + 1]
            g_row = bdt[DN_HEADS + h:DN_HEADS + h + 1, :]
            g_last = g_col[CHUNK - 1:CHUNK, :]
            decay = jnp.where(row >= col, jnp.exp(jnp.minimum(g_col - g_row, 0.0)), 0.0)
            e_g = jnp.exp(g_col)
            kb = k * beta
            kbf = k.astype(_BF16)
            a_mats.append(jnp.where(row > col, _dot_nt(kb, kbf) * decay, 0.0))
            qk_s[i] = (_dot_nt(q, kbf) * decay).astype(_BF16)
            rhs_list.append(jnp.concatenate([v * beta, kb * e_g], axis=1).astype(_BF16))
            wq_s[i, CHUNK:2 * CHUNK, :] = (q * e_g).astype(_BF16)
            kdt_s[i] = (k * jnp.exp(g_last - g_col)).T.astype(_BF16)
    t_invs = _unit_lower_inverses(a_mats, row, col)
    for c in range(n_chunks):
        rows = slice(c * CHUNK, (c + 1) * CHUNK)
        for h in range(DN_HEADS):
            i = c * DN_HEADS + h
            uw = jnp.dot(t_invs[i].astype(_BF16), rhs_list[i], preferred_element_type=_F32)
            u_s[rows, h * LANES:(h + 1) * LANES] = uw[:, :LANES]
            wq_s[i, 0:CHUNK, :] = uw[:, LANES:].astype(_BF16)

    for c in range(n_chunks):
        rows = slice(c * CHUNK, (c + 1) * CHUNK)
        last_row = grp * (nb - 1) + cpb * (c + 1) - 1
        res = []
        for h in range(DN_HEADS):
            res.append(jnp.dot(wq_s[c * DN_HEADS + h], state_s[h].astype(_BF16),
                               preferred_element_type=_F32))
        for h in range(DN_HEADS):
            i = c * DN_HEADS + h
            hc = slice(h * LANES, (h + 1) * LANES)
            v_new = (u_s[rows, hc] - res[h][:CHUNK]).astype(_BF16)
            o = res[h][CHUNK:] + jnp.dot(qk_s[i], v_new, preferred_element_type=_F32)
            for j in range(nb):
                dno_s[chunk_rows(c, j), hc] = o[cpb * j:cpb * (j + 1)]
            g_last = bd_s[last_row:last_row + 1, DN_HEADS + h:DN_HEADS + h + 1]
            state_s[h] = state_s[h] * jnp.exp(g_last) + jnp.dot(kdt_s[i], v_new,
                                                                 preferred_element_type=_F32)

    for h in range(DN_HEADS):
        hc = slice(h * LANES, (h + 1) * LANES)
        o = dno_s[:, hc]
        o = o * lax.rsqrt(jnp.mean(o * o, axis=-1, keepdims=True) + RMS_EPS) * dn_norm_ref[...]
        dno_s[:, hc] = o * zs_s[:, hc]
    y_dn = jnp.dot(dno_s[...].astype(_BF16), w_bdn_ref[...], preferred_element_type=_F32)

    y_lru = jnp.dot(ylru_s[...], w_blru_ref[...], preferred_element_type=_F32)
    merged = gates_s[:, :D_MODEL] * y_dn + gates_s[:, D_MODEL:] * y_lru
    mix = jnp.dot(merged.astype(_BF16), w_out_ref[...], preferred_element_type=_F32)
    h1 = _layer_norm(DEEPNORM_ALPHA * x + mix, ln_g_ref[...], ln_b_ref[...])
    for c in range(n_xcols):
        for j in range(nb):
            out_s[c, pl.ds(j, grp, stride=nb), :] = h1[blk(j), c * LANES:(c + 1) * LANES]
    o_ref[...] = jnp.concatenate([out_s[c] for c in range(n_xcols)], axis=1)


def _ffn_kernel(h_ref, w_gate_ref, w_up_ref, w_down_ref, ln_g_ref, ln_b_ref, o_ref):
    h = h_ref[...]
    hb = h.astype(_BF16)
    gate = jnp.dot(hb, w_gate_ref[...], preferred_element_type=_F32)
    up = jnp.dot(hb, w_up_ref[...], preferred_element_type=_F32)
    ff = jnp.dot((_silu(gate) * up).astype(_BF16), w_down_ref[...], preferred_element_type=_F32)
    o_ref[...] = _layer_norm(DEEPNORM_ALPHA * h + ff, ln_g_ref[...], ln_b_ref[...])


def _resident(shape):
    zeros = (0,) * len(shape)
    return pl.BlockSpec(shape, lambda *_: zeros, pipeline_mode=pl.Buffered(1))


def _pad_lanes(v, offset):
    return jnp.zeros((1, LANES), _F32).at[0, offset:offset + v.shape[0]].set(v.astype(_F32))


def _mixer(x, w_in, dn_conv_w, dn_a_log, dn_dt_bias, dn_norm_w, lru_conv_w, lru_conv_b,
           lru_w_a, lru_b_a, lru_w_i, lru_b_i, lru_lambda, w_branch_dn, w_branch_lru,
           b_merge_gate, w_out, ln_g, ln_b):
    bsz, seq, _ = x.shape
    tt = min(MIXER_TILE, seq)
    assert seq % tt == 0 and tt % CHUNK == 0 and CHUNK == SUBLANES * SUBLANES

    o_q, o_z = 0, 3 * DN_DIM
    o_b = o_z + DN_DIM
    o_lru = o_b + 2 * DN_HEADS
    o_mg = o_lru + 2 * LRU_WIDTH
    w_qkv = w_in[:, o_q:o_z].astype(_BF16)
    w_z = w_in[:, o_z:o_b].astype(_BF16)
    w_bd = jnp.pad(w_in[:, o_b:o_lru], ((0, 0), (0, LANES - 2 * DN_HEADS))).astype(_BF16)
    w_lru = w_in[:, o_lru:o_mg].astype(_BF16)
    w_mg = w_in[:, o_mg:].astype(_BF16)
    w_ai = jnp.concatenate([lru_w_a, lru_w_i], axis=-1).astype(_BF16)

    def row(v):
        return v.reshape(1, -1).astype(_F32)

    def x_cols(c):
        return pl.BlockSpec((None, tt, LANES), lambda b, t: (b, t, c))

    operands = [(x, x_cols(c)) for c in range(D_MODEL // LANES)] + [
        (w_qkv, None), (w_z, None), (w_bd, None), (w_lru, None), (w_mg, None),
        (dn_conv_w.astype(_F32), None),
        (_pad_lanes(dn_a_log, DN_HEADS), None), (_pad_lanes(dn_dt_bias, DN_HEADS), None),
        (row(dn_norm_w), None),
        (lru_conv_w.astype(_F32), None), (row(lru_conv_b), None), (w_ai, None),
        (row(lru_b_a), None), (row(lru_b_i), None), (row(lru_lambda), None),
        (w_branch_dn.astype(_BF16), None), (w_branch_lru.astype(_BF16), None),
        (row(b_merge_gate), None), (w_out.astype(_BF16), None), (row(ln_g), None), (row(ln_b), None),
    ]
    args = [a for a, _ in operands]
    in_specs = [spec if spec is not None else _resident(a.shape) for a, spec in operands]

    n_inst = (tt // CHUNK) * DN_HEADS
    scratch = [
        pltpu.VMEM((tt, 3 * DN_DIM), _F32),
        pltpu.VMEM((tt, LRU_WIDTH), _F32),
        pltpu.VMEM((SUBLANES, 3 * DN_DIM), _F32),
        pltpu.VMEM((SUBLANES, LRU_WIDTH), _F32),
        pltpu.VMEM((tt, LRU_WIDTH), _F32),
        pltpu.VMEM((tt, 2 * D_MODEL), _F32),
        pltpu.VMEM((tt, DN_DIM), _F32),
        pltpu.VMEM((tt, LRU_WIDTH), _BF16),
        pltpu.VMEM((tt, 3 * DN_DIM), _F32),
        pltpu.VMEM((tt, LANES), _F32),
        pltpu.VMEM((tt, DN_DIM), _F32),
        pltpu.VMEM((n_inst, 2 * CHUNK, DN_HEAD_DIM), _BF16),
        pltpu.VMEM((n_inst, CHUNK, CHUNK), _BF16),
        pltpu.VMEM((n_inst, DN_HEAD_DIM, CHUNK), _BF16),
        pltpu.VMEM((tt, DN_DIM), _F32),
        pltpu.VMEM((DN_HEADS, DN_HEAD_DIM, DN_HEAD_DIM), _F32),
        pltpu.VMEM((SUBLANES, LRU_WIDTH), _F32),
        pltpu.VMEM((D_MODEL // LANES, tt, LANES), _F32),
    ]
    return pl.pallas_call(
        _mixer_kernel,
        grid=(bsz, seq // tt),
        in_specs=in_specs,
        out_specs=pl.BlockSpec((None, tt, D_MODEL), lambda b, t: (b, t, 0)),
        out_shape=jax.ShapeDtypeStruct((bsz, seq, D_MODEL), _F32),
        scratch_shapes=scratch,
        compiler_params=pltpu.CompilerParams(
            dimension_semantics=("arbitrary", "arbitrary"), vmem_limit_bytes=VMEM_LIMIT_BYTES),
        name="hybrid_mixer_ln",
    )(*args)


def _ffn(h, w_gate, w_up, w_down, ln_g, ln_b):
    n, _ = h.shape
    tm = min(FFN_TILE, n)
    assert n % tm == 0
    args = [h, w_gate.astype(_BF16), w_up.astype(_BF16), w_down.astype(_BF16),
            ln_g.reshape(1, -1).astype(_F32), ln_b.reshape(1, -1).astype(_F32)]
    in_specs = [pl.BlockSpec((tm, D_MODEL), lambda i: (i, 0))] + [_resident(a.shape) for a in args[1:]]
    return pl.pallas_call(
        _ffn_kernel,
        grid=(n // tm,),
        in_specs=in_specs,
        out_specs=pl.BlockSpec((tm, D_MODEL), lambda i: (i, 0)),
        out_shape=jax.ShapeDtypeStruct((n, D_MODEL), _F32),
        compiler_params=pltpu.CompilerParams(
            dimension_semantics=("arbitrary",), vmem_limit_bytes=VMEM_LIMIT_BYTES),
        name="swiglu_ffn_ln",
    )(*args)


def kernel(x, w_in, dn_conv_w, dn_A_log, dn_dt_bias, dn_norm_w, lru_conv_w, lru_conv_b, lru_w_a, lru_b_a, lru_w_i, lru_b_i, lru_lambda, w_branch_dn, w_branch_lru, b_merge_gate, w_out, ln1_g, ln1_b, w_ffn_gate, w_ffn_up, w_ffn_down, ln2_g, ln2_b):
    bsz, seq, d = x.shape
    h = x
    for l in range(DEPTH):
        h = _mixer(h, w_in[l], dn_conv_w[l], dn_A_log[l], dn_dt_bias[l], dn_norm_w[l],
                   lru_conv_w[l], lru_conv_b[l], lru_w_a[l], lru_b_a[l], lru_w_i[l], lru_b_i[l],
                   lru_lambda[l], w_branch_dn[l], w_branch_lru[l], b_merge_gate[l], w_out[l],
                   ln1_g[l], ln1_b[l])
        h = _ffn(h.reshape(bsz * seq, d), w_ffn_gate[l], w_ffn_up[l], w_ffn_down[l],
                 ln2_g[l], ln2_b[l]).reshape(bsz, seq, d)
    return h
```

```python
import jax
import jax.numpy as jnp
from jax import lax
from jax.experimental import pallas as pl
from jax.experimental.pallas import tpu as pltpu

D_MODEL = 1024
DN_HEADS = 8
DN_HEAD_DIM = 128
DN_DIM = DN_HEADS * DN_HEAD_DIM
CHUNK = 64
CONV_K = 4
LRU_HEADS = 10
LRU_BLOCK = 128
LRU_WIDTH = LRU_HEADS * LRU_BLOCK
LRU_C = 8.0
DEPTH = 1
DEEPNORM_ALPHA = (2.0 * DEPTH) ** 0.25
LN_EPS = 1e-5
RMS_EPS = 1e-6

LANES = 128
SUBLANES = 8
MXU_COLS = 256
W_QKV_COL = 0
W_Z_COL = W_QKV_COL + 3 * DN_DIM
W_BD_COL = W_Z_COL + DN_DIM
W_LRU_COL = W_BD_COL + LANES
W_MG_COL = W_LRU_COL + 2 * LRU_WIDTH
MIXER_TILE = 256
FFN_TILE = 512
VMEM_LIMIT_BYTES = 56 * 1024 * 1024

_F32 = jnp.float32
_BF16 = jnp.bfloat16


def _dot_nt(a, b):
    return lax.dot_general(a.astype(_BF16), b.astype(_BF16), (((1,), (1,)), ((), ())),
                           preferred_element_type=_F32)


def _sigmoid(x):
    return 0.5 * jnp.tanh(0.5 * x) + 0.5


def _silu(x):
    h = 0.5 * x
    return h * jnp.tanh(h) + h


def _softplus(x):
    return jnp.maximum(x, 0.0) + jnp.log1p(jnp.exp(-jnp.abs(x)))


def _gelu_tanh(x):
    c = 0.7978845608028654
    return 0.5 * x * (1.0 + jnp.tanh(c * (x + 0.044715 * (x * x * x))))


def _layer_norm(x, g, b):
    mu = jnp.mean(x, axis=-1, keepdims=True)
    xc = x - mu
    var = jnp.mean(xc * xc, axis=-1, keepdims=True)
    return xc * lax.rsqrt(var + LN_EPS) * g + b


def _unit_lower_inverses(a_list, row, col):
    n = a_list[0].shape[0]
    eye = (row == col).astype(_F32)
    first = (row == col + 1) & ((col & 1) == 0)
    xs = [eye - jnp.where(first, a, 0.0) for a in a_list]
    s = 2
    while s < n:
        sh = s.bit_length() - 1
        bi = row >> sh
        bj = col >> sh
        join = (bi == bj + 1) & ((bi & 1) == 1)
        xbs = [x.astype(_BF16) for x in xs]
        ys = [jnp.dot(xb, jnp.where(join, a, 0.0).astype(_BF16), preferred_element_type=_F32)
              for xb, a in zip(xbs, a_list)]
        xs = [x - jnp.dot(y.astype(_BF16), xb, preferred_element_type=_F32)
              for x, y, xb in zip(xs, ys, xbs)]
        s *= 2
    return xs


def _mixer_kernel(*refs):
    nb = SUBLANES
    n_xcols = D_MODEL // LANES
    x_refs = refs[:n_xcols]
    (w_in_ref,
     dn_conv_ref, a_log_ref, dt_bias_ref, dn_norm_ref,
     lru_conv_w_ref, lru_conv_b_ref, w_ai_ref, b_a_ref, b_i_ref, lam_ref,
     w_bdn_ref, w_blru_ref, b_mg_ref, w_out_ref, ln_g_ref, ln_b_ref,
     o_ref,
     qkv_s, lrux_s, qkv_tail_s, lrux_tail_s, lrug_s, gates_s, zs_s, ylru_s, act_s, bd_s, u_s,
     wq_s, qk_s, kdt_s, dno_s, state_s, hl_s, out_s) = refs[n_xcols:]
    tt = x_refs[0].shape[0]
    grp = tt // nb
    cpb = CHUNK // nb
    n_chunks = tt // CHUNK
    t_idx = pl.program_id(1)

    def blk(j):
        return slice(grp * j, grp * (j + 1))

    def chunk_rows(c, j):
        return slice(grp * j + cpb * c, grp * j + cpb * (c + 1))

    def gather_chunk(ref, c, cols):
        return jnp.concatenate([ref[chunk_rows(c, j), cols] for j in range(nb)], axis=0)

    @pl.when(t_idx == 0)
    def _():
        qkv_tail_s[...] = jnp.zeros(qkv_tail_s.shape, _F32)
        lrux_tail_s[...] = jnp.zeros(lrux_tail_s.shape, _F32)
        state_s[...] = jnp.zeros(state_s.shape, _F32)
        hl_s[...] = jnp.zeros(hl_s.shape, _F32)

    x = jnp.concatenate(
        [jnp.concatenate([x_ref[pl.ds(j, grp, stride=nb), :] for x_ref in x_refs], axis=1)
         for j in range(nb)], axis=0)
    xb = x.astype(_BF16)

    row_g = lax.broadcasted_iota(jnp.int32, (grp, LANES), 0)
    row_t = lax.broadcasted_iota(jnp.int32, (tt, LANES), 0)
    first_pos = (row_t == 0) & (t_idx == 0)
    sub3 = lax.broadcasted_iota(jnp.int32, (grp // SUBLANES, SUBLANES, LANES), 1)
    neg_c_softplus = -LRU_C * _softplus(-lam_ref[...])

    def causal_conv(raw_ref, tail_ref, w_ref, cols):
        blocks = [raw_ref[blk(j), cols] for j in range(nb)]
        wrapped = {j: jnp.where(row_g == 0, tail_ref[j:j + 1, cols], pltpu.roll(blocks[j], 1, 0))
                   for j in range(nb - (CONV_K - 1), nb)}
        outs = []
        for j in range(nb):
            acc = blocks[j] * w_ref[CONV_K - 1:CONV_K, cols]
            for k in range(CONV_K - 1):
                back = CONV_K - 1 - k
                src = blocks[j - back] if j >= back else wrapped[j - back + nb]
                acc = acc + src * w_ref[k:k + 1, cols]
            outs.append(acc)
        return jnp.concatenate(outs, axis=0)

    def save_tail(raw_ref, tail_ref):
        for j in range(nb - (CONV_K - 1), nb):
            tail_ref[j:j + 1, :] = raw_ref[grp * (j + 1) - 1:grp * (j + 1), :]

    def proj_tile(base, i):
        cs = slice(i * MXU_COLS, (i + 1) * MXU_COLS)
        w = w_in_ref[:, base + i * MXU_COLS:base + (i + 1) * MXU_COLS]
        return cs, jnp.dot(xb, w, preferred_element_type=_F32)

    def qkv_task(i):
        cs, p = proj_tile(W_QKV_COL, i)
        qkv_s[:, cs] = p

    def lru_task(i):
        cs, p = proj_tile(W_LRU_COL, i)
        n_x = LRU_WIDTH // MXU_COLS
        if i < n_x:
            lrux_s[:, cs] = p
        else:
            lrug_s[:, (i - n_x) * MXU_COLS:(i - n_x + 1) * MXU_COLS] = p

    def mg_task(i):
        cs, p = proj_tile(W_MG_COL, i)
        gates_s[:, cs] = _sigmoid(p + b_mg_ref[:, cs])

    def z_task(i):
        cs, p = proj_tile(W_Z_COL, i)
        zs_s[:, cs] = _silu(p)

    def dn_conv_task(j):
        cols = slice(j * LANES, (j + 1) * LANES)
        a = _silu(causal_conv(qkv_s, qkv_tail_s, dn_conv_ref, cols))
        if j < 2 * DN_HEADS:
            a = a * lax.rsqrt(jnp.sum(a * a, axis=-1, keepdims=True) + RMS_EPS)
        if j < DN_HEADS:
            a = a * (DN_HEAD_DIM ** -0.5)
        act_s[:, cols] = a

    def lru_block_task(jh):
        cols = slice(jh * LANES, (jh + 1) * LANES)
        u = causal_conv(lrux_s, lrux_tail_s, lru_conv_w_ref, cols) + lru_conv_b_ref[:, cols]
        ai = jnp.dot(u.astype(_BF16), w_ai_ref[jh], preferred_element_type=_F32)
        r = _sigmoid(ai[:, :LANES] + b_a_ref[:, cols])
        i_gate = _sigmoid(ai[:, LANES:] + b_i_ref[:, cols])
        log_a = r * neg_c_softplus[:, cols]
        a = jnp.exp(log_a)
        one_minus_a2 = 1.0 - jnp.exp(2.0 * log_a)
        mult = jnp.where(one_minus_a2 > 0.0, one_minus_a2 * lax.rsqrt(one_minus_a2), 0.0)
        mult = jnp.where(first_pos, 1.0, mult)
        b = mult * (i_gate * u)
        h_loc, a_cum = [b[blk(0)]], [a[blk(0)]]
        for j in range(1, nb):
            h_loc.append(a[blk(j)] * h_loc[-1] + b[blk(j)])
            a_cum.append(a[blk(j)] * a_cum[-1])
        a3 = a_cum[nb - 1].reshape(grp // SUBLANES, SUBLANES, LANES)
        h3 = h_loc[nb - 1].reshape(grp // SUBLANES, SUBLANES, LANES)
        s = 1
        while s < SUBLANES:
            keep = sub3 >= s
            h3 = h3 + jnp.where(keep, a3 * pltpu.roll(h3, s, 1), 0.0)
            a3 = a3 * jnp.where(keep, pltpu.roll(a3, s, 1), 1.0)
            s *= 2
        h_prev = hl_s[0:1, cols]
        carry = h_prev
        row_end = []
        for i in range(grp // SUBLANES):
            e = h3[i] + a3[i] * carry
            carry = e[SUBLANES - 1:SUBLANES, :]
            row_end.append(e)
        hl_s[0:1, cols] = carry
        row_end = jnp.concatenate(row_end, axis=0)
        h_in = jnp.where(row_g == 0, h_prev, pltpu.roll(row_end, 1, 0))
        hcur = jnp.concatenate([h_loc[j] + a_cum[j] * h_in for j in range(nb)], axis=0)
        ylru_s[:, cols] = (hcur * _gelu_tanh(lrug_s[:, cols])).astype(_BF16)

    def tasks(fn, n):
        return [lambda i=i: fn(i) for i in range(n)]

    mxu_tasks = (tasks(qkv_task, 3 * DN_DIM // MXU_COLS) + tasks(lru_task, 2 * LRU_WIDTH // MXU_COLS)
                 + tasks(mg_task, 2 * D_MODEL // MXU_COLS) + tasks(z_task, DN_DIM // MXU_COLS))
    vpu_tasks = tasks(dn_conv_task, 3 * DN_HEADS) + tasks(lru_block_task, LRU_HEADS)
    mxu_tasks[0]()
    for i in range(max(len(mxu_tasks) - 1, len(vpu_tasks))):
        if i + 1 < len(mxu_tasks):
            mxu_tasks[i + 1]()
        if i < len(vpu_tasks):
            vpu_tasks[i]()
    save_tail(qkv_s, qkv_tail_s)
    save_tail(lrux_s, lrux_tail_s)

    bd = jnp.dot(xb, w_in_ref[:, W_BD_COL:W_BD_COL + LANES], preferred_element_type=_F32)
    beta_all = _sigmoid(bd)
    g_all = -jnp.exp(a_log_ref[...]) * _softplus(bd + dt_bias_ref[...])
    prefix = [g_all[blk(0)]]
    for j in range(1, nb):
        prefix.append(prefix[-1] + g_all[blk(j)])
    row_total = prefix[nb - 1]
    row_in_chunk = row_g & (cpb - 1)
    incl = row_total
    s = 1
    while s < cpb:
        incl = incl + jnp.where(row_in_chunk >= s, pltpu.roll(incl, s, 0), 0.0)
        s *= 2
    before_row = incl - row_total
    g_cum = jnp.concatenate([p + before_row for p in prefix], axis=0)
    lane = lax.broadcasted_iota(jnp.int32, (tt, LANES), 1)
    bd_s[...] = jnp.where(lane < DN_HEADS, beta_all, g_cum)

    cpb_shift = cpb.bit_length() - 1
    nb_shift = nb.bit_length() - 1
    r_idx = lax.broadcasted_iota(jnp.int32, (CHUNK, CHUNK), 0)
    c_idx = lax.broadcasted_iota(jnp.int32, (CHUNK, CHUNK), 1)
    row = (r_idx >> cpb_shift) + ((r_idx & (cpb - 1)) << nb_shift)
    col = (c_idx >> cpb_shift) + ((c_idx & (cpb - 1)) << nb_shift)

    a_mats, rhs_list = [], []
    for c in range(n_chunks):
        bdc = gather_chunk(bd_s, c, slice(None))
        bdt = bdc.T
        for h in range(DN_HEADS):
            i = c * DN_HEADS + h
            q = gather_chunk(act_s, c, slice(h * LANES, (h + 1) * LANES))
            k = gather_chunk(act_s, c, slice(DN_DIM + h * LANES, DN_DIM + (h + 1) * LANES))
            v = gather_chunk(act_s, c, slice(2 * DN_DIM + h * LANES, 2 * DN_DIM + (h + 1) * LANES))
            beta = bdc[:, h:h + 1]
            g_col = bdc[:, DN_HEADS + h:DN_HEADS + h + 1]
            g_row = bdt[DN_HEADS + h:DN_HEADS + h + 1, :]
            g_last = g_col[CHUNK - 1:CHUNK, :]
            decay = jnp.where(row >= col, jnp.exp(jnp.minimum(g_col - g_row, 0.0)), 0.0)
            e_g = jnp.exp(g_col)
            kb = k * beta
            kbf = k.astype(_BF16)
            kq = _dot_nt(jnp.concatenate([kb, q], axis=0), kbf)
            a_mats.append(jnp.where(row > col, kq[:CHUNK] * decay, 0.0))
            qk_s[i] = (kq[CHUNK:] * decay).astype(_BF16)
            rhs_list.append(jnp.concatenate([v * beta, kb * e_g], axis=1).astype(_BF16))
            wq_s[i, CHUNK:2 * CHUNK, :] = (q * e_g).astype(_BF16)
            kdt_s[i] = (k * jnp.exp(g_last - g_col)).T.astype(_BF16)
    t_invs = _unit_lower_inverses(a_mats, row, col)
    for c in range(n_chunks):
        rows = slice(c * CHUNK, (c + 1) * CHUNK)
        for h in range(DN_HEADS):
            i = c * DN_HEADS + h
            uw = jnp.dot(t_invs[i].astype(_BF16), rhs_list[i], preferred_element_type=_F32)
            u_s[rows, h * LANES:(h + 1) * LANES] = uw[:, :LANES]
            wq_s[i, 0:CHUNK, :] = uw[:, LANES:].astype(_BF16)

    for c in range(n_chunks):
        rows = slice(c * CHUNK, (c + 1) * CHUNK)
        last_row = grp * (nb - 1) + cpb * (c + 1) - 1
        res = []
        for h in range(DN_HEADS):
            res.append(jnp.dot(wq_s[c * DN_HEADS + h], state_s[h].astype(_BF16),
                               preferred_element_type=_F32))
        for h in range(DN_HEADS):
            i = c * DN_HEADS + h
            hc = slice(h * LANES, (h + 1) * LANES)
            v_new = (u_s[rows, hc] - res[h][:CHUNK]).astype(_BF16)
            o = res[h][CHUNK:] + jnp.dot(qk_s[i], v_new, preferred_element_type=_F32)
            for j in range(nb):
                dno_s[chunk_rows(c, j), hc] = o[cpb * j:cpb * (j + 1)]
            g_last = bd_s[last_row:last_row + 1, DN_HEADS + h:DN_HEADS + h + 1]
            state_s[h] = state_s[h] * jnp.exp(g_last) + jnp.dot(kdt_s[i], v_new,
                                                                 preferred_element_type=_F32)

    for h in range(DN_HEADS):
        hc = slice(h * LANES, (h + 1) * LANES)
        o = dno_s[:, hc]
        o = o * lax.rsqrt(jnp.mean(o * o, axis=-1, keepdims=True) + RMS_EPS) * dn_norm_ref[...]
        dno_s[:, hc] = o * zs_s[:, hc]
    y_dn = jnp.dot(dno_s[...].astype(_BF16), w_bdn_ref[...], preferred_element_type=_F32)

    y_lru = jnp.dot(ylru_s[...], w_blru_ref[...], preferred_element_type=_F32)
    merged = gates_s[:, :D_MODEL] * y_dn + gates_s[:, D_MODEL:] * y_lru
    mix = jnp.dot(merged.astype(_BF16), w_out_ref[...], preferred_element_type=_F32)
    h1 = _layer_norm(DEEPNORM_ALPHA * x + mix, ln_g_ref[...], ln_b_ref[...])
    for c in range(n_xcols):
        for j in range(nb):
            out_s[c, pl.ds(j, grp, stride=nb), :] = h1[blk(j), c * LANES:(c + 1) * LANES]
    o_ref[...] = jnp.concatenate([out_s[c] for c in range(n_xcols)], axis=1)


def _ffn_kernel(h_ref, w_gate_ref, w_up_ref, w_down_ref, ln_g_ref, ln_b_ref, o_ref):
    h = h_ref[...]
    hb = h.astype(_BF16)
    gate = jnp.dot(hb, w_gate_ref[...], preferred_element_type=_F32)
    up = jnp.dot(hb, w_up_ref[...], preferred_element_type=_F32)
    ff = jnp.dot((_silu(gate) * up).astype(_BF16), w_down_ref[...], preferred_element_type=_F32)
    o_ref[...] = _layer_norm(DEEPNORM_ALPHA * h + ff, ln_g_ref[...], ln_b_ref[...])


def _resident(shape):
    zeros = (0,) * len(shape)
    return pl.BlockSpec(shape, lambda *_: zeros, pipeline_mode=pl.Buffered(1))


def _pad_lanes(v, offset):
    return jnp.zeros((1, LANES), _F32).at[0, offset:offset + v.shape[0]].set(v.astype(_F32))


def _mixer(x, w_in, dn_conv_w, dn_a_log, dn_dt_bias, dn_norm_w, lru_conv_w, lru_conv_b,
           lru_w_a, lru_b_a, lru_w_i, lru_b_i, lru_lambda, w_branch_dn, w_branch_lru,
           b_merge_gate, w_out, ln_g, ln_b):
    bsz, seq, _ = x.shape
    tt = min(MIXER_TILE, seq)
    assert seq % tt == 0 and tt % CHUNK == 0 and CHUNK == SUBLANES * SUBLANES

    n_head_cols = W_BD_COL + 2 * DN_HEADS
    lane_pad = jnp.zeros((D_MODEL, LANES - 2 * DN_HEADS), w_in.dtype)
    w_in_packed = jnp.concatenate([w_in[:, :n_head_cols], lane_pad, w_in[:, n_head_cols:]],
                                  axis=1).astype(_BF16)
    w_ai = jnp.concatenate([lru_w_a, lru_w_i], axis=-1).astype(_BF16)

    def row(v):
        return v.reshape(1, -1).astype(_F32)

    def x_cols(c):
        return pl.BlockSpec((None, tt, LANES), lambda b, t: (b, t, c))

    operands = [(x, x_cols(c)) for c in range(D_MODEL // LANES)] + [
        (w_in_packed, None),
        (dn_conv_w.astype(_F32), None),
        (_pad_lanes(dn_a_log, DN_HEADS), None), (_pad_lanes(dn_dt_bias, DN_HEADS), None),
        (row(dn_norm_w), None),
        (lru_conv_w.astype(_F32), None), (row(lru_conv_b), None), (w_ai, None),
        (row(lru_b_a), None), (row(lru_b_i), None), (row(lru_lambda), None),
        (w_branch_dn.astype(_BF16), None), (w_branch_lru.astype(_BF16), None),
        (row(b_merge_gate), None), (w_out.astype(_BF16), None), (row(ln_g), None), (row(ln_b), None),
    ]
    args = [a for a, _ in operands]
    in_specs = [spec if spec is not None else _resident(a.shape) for a, spec in operands]

    n_inst = (tt // CHUNK) * DN_HEADS
    scratch = [
        pltpu.VMEM((tt, 3 * DN_DIM), _F32),
        pltpu.VMEM((tt, LRU_WIDTH), _F32),
        pltpu.VMEM((SUBLANES, 3 * DN_DIM), _F32),
        pltpu.VMEM((SUBLANES, LRU_WIDTH), _F32),
        pltpu.VMEM((tt, LRU_WIDTH), _F32),
        pltpu.VMEM((tt, 2 * D_MODEL), _F32),
        pltpu.VMEM((tt, DN_DIM), _F32),
        pltpu.VMEM((tt, LRU_WIDTH), _BF16),
        pltpu.VMEM((tt, 3 * DN_DIM), _F32),
        pltpu.VMEM((tt, LANES), _F32),
        pltpu.VMEM((tt, DN_DIM), _F32),
        pltpu.VMEM((n_inst, 2 * CHUNK, DN_HEAD_DIM), _BF16),
        pltpu.VMEM((n_inst, CHUNK, CHUNK), _BF16),
        pltpu.VMEM((n_inst, DN_HEAD_DIM, CHUNK), _BF16),
        pltpu.VMEM((tt, DN_DIM), _F32),
        pltpu.VMEM((DN_HEADS, DN_HEAD_DIM, DN_HEAD_DIM), _F32),
        pltpu.VMEM((SUBLANES, LRU_WIDTH), _F32),
        pltpu.VMEM((D_MODEL // LANES, tt, LANES), _F32),
    ]
    return pl.pallas_call(
        _mixer_kernel,
        grid=(bsz, seq // tt),
        in_specs=in_specs,
        out_specs=pl.BlockSpec((None, tt, D_MODEL), lambda b, t: (b, t, 0)),
        out_shape=jax.ShapeDtypeStruct((bsz, seq, D_MODEL), _F32),
        scratch_shapes=scratch,
        compiler_params=pltpu.CompilerParams(
            dimension_semantics=("arbitrary", "arbitrary"), vmem_limit_bytes=VMEM_LIMIT_BYTES),
        name="hybrid_mixer_ln",
    )(*args)


def _ffn(h, w_gate, w_up, w_down, ln_g, ln_b):
    n, _ = h.shape
    tm = min(FFN_TILE, n)
    assert n % tm == 0
    args = [h, w_gate.astype(_BF16), w_up.astype(_BF16), w_down.astype(_BF16),
            ln_g.reshape(1, -1).astype(_F32), ln_b.reshape(1, -1).astype(_F32)]
    in_specs = [pl.BlockSpec((tm, D_MODEL), lambda i: (i, 0))] + [_resident(a.shape) for a in args[1:]]
    return pl.pallas_call(
        _ffn_kernel,
        grid=(n // tm,),
        in_specs=in_specs,
        out_specs=pl.BlockSpec((tm, D_MODEL), lambda i: (i, 0)),
        out_shape=jax.ShapeDtypeStruct((n, D_MODEL), _F32),
        compiler_params=pltpu.CompilerParams(
            dimension_semantics=("arbitrary",), vmem_limit_bytes=VMEM_LIMIT_BYTES),
        name="swiglu_ffn_ln",
    )(*args)


def kernel(x, w_in, dn_conv_w, dn_A_log, dn_dt_bias, dn_norm_w, lru_conv_w, lru_conv_b, lru_w_a, lru_b_a, lru_w_i, lru_b_i, lru_lambda, w_branch_dn, w_branch_lru, b_merge_gate, w_out, ln1_g, ln1_b, w_ffn_gate, w_ffn_up, w_ffn_down, ln2_g, ln2_b):
    bsz, seq, d = x.shape
    h = x
    for l in range(DEPTH):
        h = _mixer(h, w_in[l], dn_conv_w[l], dn_A_log[l], dn_dt_bias[l], dn_norm_w[l],
                   lru_conv_w[l], lru_conv_b[l], lru_w_a[l], lru_b_a[l], lru_w_i[l], lru_b_i[l],
                   lru_lambda[l], w_branch_dn[l], w_branch_lru[l], b_merge_gate[l], w_out[l],
                   ln1_g[l], ln1_b[l])
        h = _ffn(h.reshape(bsz * seq, d), w_ffn_gate[l], w_ffn_up[l], w_ffn_down[l],
                 ln2_g[l], ln2_b[l]).reshape(bsz, seq, d)
    return h
```

```python
import jax
import jax.numpy as jnp
from jax import lax
from jax.experimental import pallas as pl
from jax.experimental.pallas import tpu as pltpu

D_MODEL = 1024
DN_HEADS = 8
DN_HEAD_DIM = 128
DN_DIM = DN_HEADS * DN_HEAD_DIM
CHUNK = 64
CONV_K = 4
LRU_HEADS = 10
LRU_BLOCK = 128
LRU_WIDTH = LRU_HEADS * LRU_BLOCK
LRU_C = 8.0
DEPTH = 1
DEEPNORM_ALPHA = (2.0 * DEPTH) ** 0.25
LN_EPS = 1e-5
RMS_EPS = 1e-6

LANES = 128
SUBLANES = 8
MXU_COLS = 256
W_QKV_COL = 0
W_Z_COL = W_QKV_COL + 3 * DN_DIM
W_BD_COL = W_Z_COL + DN_DIM
W_LRU_COL = W_BD_COL + LANES
W_MG_COL = W_LRU_COL + 2 * LRU_WIDTH
MIXER_TILE = 256
FFN_TILE = 512
VMEM_LIMIT_BYTES = 56 * 1024 * 1024

_F32 = jnp.float32
_BF16 = jnp.bfloat16


def _dot_nt(a, b):
    return lax.dot_general(a.astype(_BF16), b.astype(_BF16), (((1,), (1,)), ((), ())),
                           preferred_element_type=_F32)


def _sigmoid(x):
    return 0.5 * jnp.tanh(0.5 * x) + 0.5


def _silu(x):
    h = 0.5 * x
    return h * jnp.tanh(h) + h


def _softplus(x):
    return jnp.maximum(x, 0.0) + jnp.log1p(jnp.exp(-jnp.abs(x)))


def _gelu_tanh(x):
    c = 0.7978845608028654
    return 0.5 * x * (1.0 + jnp.tanh(c * (x + 0.044715 * (x * x * x))))


def _layer_norm(x, g, b):
    mu = jnp.mean(x, axis=-1, keepdims=True)
    xc = x - mu
    var = jnp.mean(xc * xc, axis=-1, keepdims=True)
    return xc * lax.rsqrt(var + LN_EPS) * g + b


def _unit_lower_inverses(a_list, row, col):
    n = a_list[0].shape[0]
    eye = (row == col).astype(_F32)
    first = (row == col + 1) & ((col & 1) == 0)
    xs = [eye - jnp.where(first, a, 0.0) for a in a_list]
    s = 2
    while s < n:
        sh = s.bit_length() - 1
        bi = row >> sh
        bj = col >> sh
        join = (bi == bj + 1) & ((bi & 1) == 1)
        xbs = [x.astype(_BF16) for x in xs]
        ys = [jnp.dot(xb, jnp.where(join, a, 0.0).astype(_BF16), preferred_element_type=_F32)
              for xb, a in zip(xbs, a_list)]
        xs = [x - jnp.dot(y.astype(_BF16), xb, preferred_element_type=_F32)
              for x, y, xb in zip(xs, ys, xbs)]
        s *= 2
    return xs


def _mixer_kernel(*refs):
    nb = SUBLANES
    n_xcols = D_MODEL // LANES
    x_refs = refs[:n_xcols]
    (w_in_ref,
     dn_conv_ref, a_log_ref, dt_bias_ref, dn_norm_ref,
     lru_conv_w_ref, lru_conv_b_ref, w_ai_ref, b_a_ref, b_i_ref, lam_ref,
     w_bdn_ref, w_blru_ref, b_mg_ref, w_out_ref, ln_g_ref, ln_b_ref,
     o_ref,
     qkv_s, lrux_s, qkv_tail_s, lrux_tail_s, lrug_s, gates_s, zs_s, ylru_s, act_s, bd_s, u_s,
     wq_s, qk_s, kdt_s, dno_s, state_s, hl_s, out_s) = refs[n_xcols:]
    tt = x_refs[0].shape[0]
    grp = tt // nb
    cpb = CHUNK // nb
    n_chunks = tt // CHUNK
    t_idx = pl.program_id(1)

    def blk(j):
        return slice(grp * j, grp * (j + 1))

    def chunk_rows(c, j):
        return slice(grp * j + cpb * c, grp * j + cpb * (c + 1))

    def gather_chunk(ref, c, cols):
        return jnp.concatenate([ref[chunk_rows(c, j), cols] for j in range(nb)], axis=0)

    @pl.when(t_idx == 0)
    def _():
        qkv_tail_s[...] = jnp.zeros(qkv_tail_s.shape, _F32)
        lrux_tail_s[...] = jnp.zeros(lrux_tail_s.shape, _F32)
        state_s[...] = jnp.zeros(state_s.shape, _F32)
        hl_s[...] = jnp.zeros(hl_s.shape, _F32)

    x = jnp.concatenate(
        [jnp.concatenate([x_ref[pl.ds(j, grp, stride=nb), :] for x_ref in x_refs], axis=1)
         for j in range(nb)], axis=0)
    xb = x.astype(_BF16)

    row_g = lax.broadcasted_iota(jnp.int32, (grp, LANES), 0)
    row_t = lax.broadcasted_iota(jnp.int32, (tt, LANES), 0)
    first_pos = (row_t == 0) & (t_idx == 0)
    sub3 = lax.broadcasted_iota(jnp.int32, (grp // SUBLANES, SUBLANES, LANES), 1)
    neg_c_softplus = -LRU_C * _softplus(-lam_ref[...])

    def causal_conv(raw_ref, tail_ref, w_ref, cols):
        blocks = [raw_ref[blk(j), cols] for j in range(nb)]
        wrapped = {j: jnp.where(row_g == 0, tail_ref[j:j + 1, cols], pltpu.roll(blocks[j], 1, 0))
                   for j in range(nb - (CONV_K - 1), nb)}
        outs = []
        for j in range(nb):
            acc = blocks[j] * w_ref[CONV_K - 1:CONV_K, cols]
            for k in range(CONV_K - 1):
                back = CONV_K - 1 - k
                src = blocks[j - back] if j >= back else wrapped[j - back + nb]
                acc = acc + src * w_ref[k:k + 1, cols]
            outs.append(acc)
        return jnp.concatenate(outs, axis=0)

    def save_tail(raw_ref, tail_ref):
        for j in range(nb - (CONV_K - 1), nb):
            tail_ref[j:j + 1, :] = raw_ref[grp * (j + 1) - 1:grp * (j + 1), :]

    def proj_tile(base, i):
        cs = slice(i * MXU_COLS, (i + 1) * MXU_COLS)
        w = w_in_ref[:, base + i * MXU_COLS:base + (i + 1) * MXU_COLS]
        return cs, jnp.dot(xb, w, preferred_element_type=_F32)

    def qkv_task(i):
        cs, p = proj_tile(W_QKV_COL, i)
        qkv_s[:, cs] = p

    def lru_task(i):
        cs, p = proj_tile(W_LRU_COL, i)
        n_x = LRU_WIDTH // MXU_COLS
        if i < n_x:
            lrux_s[:, cs] = p
        else:
            lrug_s[:, (i - n_x) * MXU_COLS:(i - n_x + 1) * MXU_COLS] = p

    def mg_task(i):
        cs, p = proj_tile(W_MG_COL, i)
        gates_s[:, cs] = _sigmoid(p + b_mg_ref[:, cs])

    def z_task(i):
        cs, p = proj_tile(W_Z_COL, i)
        zs_s[:, cs] = _silu(p)

    def dn_conv_task(j):
        cols = slice(j * LANES, (j + 1) * LANES)
        a = _silu(causal_conv(qkv_s, qkv_tail_s, dn_conv_ref, cols))
        if j < 2 * DN_HEADS:
            a = a * lax.rsqrt(jnp.sum(a * a, axis=-1, keepdims=True) + RMS_EPS)
        if j < DN_HEADS:
            a = a * (DN_HEAD_DIM ** -0.5)
        act_s[:, cols] = a

    def lru_block_task(jh):
        cols = slice(jh * LANES, (jh + 1) * LANES)
        u = causal_conv(lrux_s, lrux_tail_s, lru_conv_w_ref, cols) + lru_conv_b_ref[:, cols]
        ai = jnp.dot(u.astype(_BF16), w_ai_ref[jh], preferred_element_type=_F32)
        r = _sigmoid(ai[:, :LANES] + b_a_ref[:, cols])
        i_gate = _sigmoid(ai[:, LANES:] + b_i_ref[:, cols])
        log_a = r * neg_c_softplus[:, cols]
        a = jnp.exp(log_a)
        one_minus_a2 = 1.0 - jnp.exp(2.0 * log_a)
        mult = jnp.where(one_minus_a2 > 0.0, one_minus_a2 * lax.rsqrt(one_minus_a2), 0.0)
        mult = jnp.where(first_pos, 1.0, mult)
        b = mult * (i_gate * u)
        h_loc, a_cum = [b[blk(0)]], [a[blk(0)]]
        for j in range(1, nb):
            h_loc.append(a[blk(j)] * h_loc[-1] + b[blk(j)])
            a_cum.append(a[blk(j)] * a_cum[-1])
        a3 = a_cum[nb - 1].reshape(grp // SUBLANES, SUBLANES, LANES)
        h3 = h_loc[nb - 1].reshape(grp // SUBLANES, SUBLANES, LANES)
        s = 1
        while s < SUBLANES:
            keep = sub3 >= s
            h3 = h3 + jnp.where(keep, a3 * pltpu.roll(h3, s, 1), 0.0)
            a3 = a3 * jnp.where(keep, pltpu.roll(a3, s, 1), 1.0)
            s *= 2
        h_prev = hl_s[0:1, cols]
        carry = h_prev
        row_end = []
        for i in range(grp // SUBLANES):
            e = h3[i] + a3[i] * carry
            carry = e[SUBLANES - 1:SUBLANES, :]
            row_end.append(e)
        hl_s[0:1, cols] = carry
        row_end = jnp.concatenate(row_end, axis=0)
        h_in = jnp.where(row_g == 0, h_prev, pltpu.roll(row_end, 1, 0))
        hcur = jnp.concatenate([h_loc[j] + a_cum[j] * h_in for j in range(nb)], axis=0)
        ylru_s[:, cols] = (hcur * _gelu_tanh(lrug_s[:, cols])).astype(_BF16)

    def tasks(fn, n):
        return [lambda i=i: fn(i) for i in range(n)]

    mxu_tasks = (tasks(qkv_task, 3 * DN_DIM // MXU_COLS) + tasks(lru_task, 2 * LRU_WIDTH // MXU_COLS)
                 + tasks(mg_task, 2 * D_MODEL // MXU_COLS) + tasks(z_task, DN_DIM // MXU_COLS))
    vpu_tasks = tasks(dn_conv_task, 3 * DN_HEADS) + tasks(lru_block_task, LRU_HEADS)
    mxu_tasks[0]()
    for i in range(max(len(mxu_tasks) - 1, len(vpu_tasks))):
        if i + 1 < len(mxu_tasks):
            mxu_tasks[i + 1]()
        if i < len(vpu_tasks):
            vpu_tasks[i]()
    save_tail(qkv_s, qkv_tail_s)
    save_tail(lrux_s, lrux_tail_s)

    bd = jnp.dot(xb, w_in_ref[:, W_BD_COL:W_BD_COL + LANES], preferred_element_type=_F32)
    beta_all = _sigmoid(bd)
    g_all = -jnp.exp(a_log_ref[...]) * _softplus(bd + dt_bias_ref[...])
    prefix = [g_all[blk(0)]]
    for j in range(1, nb):
        prefix.append(prefix[-1] + g_all[blk(j)])
    row_total = prefix[nb - 1]
    row_in_chunk = row_g & (cpb - 1)
    incl = row_total
    s = 1
    while s < cpb:
        incl = incl + jnp.where(row_in_chunk >= s, pltpu.roll(incl, s, 0), 0.0)
        s *= 2
    before_row = incl - row_total
    g_cum = jnp.concatenate([p + before_row for p in prefix], axis=0)
    lane = lax.broadcasted_iota(jnp.int32, (tt, LANES), 1)
    bd_s[...] = jnp.where(lane < DN_HEADS, beta_all, g_cum)

    cpb_shift = cpb.bit_length() - 1
    nb_shift = nb.bit_length() - 1
    r_idx = lax.broadcasted_iota(jnp.int32, (CHUNK, CHUNK), 0)
    c_idx = lax.broadcasted_iota(jnp.int32, (CHUNK, CHUNK), 1)
    row = (r_idx >> cpb_shift) + ((r_idx & (cpb - 1)) << nb_shift)
    col = (c_idx >> cpb_shift) + ((c_idx & (cpb - 1)) << nb_shift)

    a_mats, rhs_list = [], []
    for c in range(n_chunks):
        bdc = gather_chunk(bd_s, c, slice(None))
        bdt = bdc.T
        for h in range(DN_HEADS):
            i = c * DN_HEADS + h
            q = gather_chunk(act_s, c, slice(h * LANES, (h + 1) * LANES))
            k = gather_chunk(act_s, c, slice(DN_DIM + h * LANES, DN_DIM + (h + 1) * LANES))
            v = gather_chunk(act_s, c, slice(2 * DN_DIM + h * LANES, 2 * DN_DIM + (h + 1) * LANES))
            beta = bdc[:, h:h + 1]
            g_col = bdc[:, DN_HEADS + h:DN_HEADS + h + 1]
            g_row = bdt[DN_HEADS + h:DN_HEADS + h + 1, :]
            g_last = g_col[CHUNK - 1:CHUNK, :]
            decay = jnp.where(row >= col, jnp.exp(jnp.minimum(g_col - g_row, 0.0)), 0.0)
            e_g = jnp.exp(g_col)
            kb = k * beta
            kbf = k.astype(_BF16)
            kq = _dot_nt(jnp.concatenate([kb, q], axis=0), kbf)
            a_mats.append(jnp.where(row > col, kq[:CHUNK] * decay, 0.0))
            qk_s[i] = (kq[CHUNK:] * decay).astype(_BF16)
            rhs_list.append(jnp.concatenate([v * beta, kb * e_g], axis=1).astype(_BF16))
            wq_s[i, CHUNK:2 * CHUNK, :] = (q * e_g).astype(_BF16)
            kdt_s[i] = (k * jnp.exp(g_last - g_col)).T.astype(_BF16)
    t_invs = _unit_lower_inverses(a_mats, row, col)
    for c in range(n_chunks):
        rows = slice(c * CHUNK, (c + 1) * CHUNK)
        for h in range(DN_HEADS):
            i = c * DN_HEADS + h
            uw = jnp.dot(t_invs[i].astype(_BF16), rhs_list[i], preferred_element_type=_F32)
            u_s[rows, h * LANES:(h + 1) * LANES] = uw[:, :LANES]
            wq_s[i, 0:CHUNK, :] = uw[:, LANES:].astype(_BF16)

    for c in range(n_chunks):
        rows = slice(c * CHUNK, (c + 1) * CHUNK)
        last_row = grp * (nb - 1) + cpb * (c + 1) - 1
        res = []
        for h in range(DN_HEADS):
            res.append(jnp.dot(wq_s[c * DN_HEADS + h], state_s[h].astype(_BF16),
                               preferred_element_type=_F32))
        for h in range(DN_HEADS):
            i = c * DN_HEADS + h
            hc = slice(h * LANES, (h + 1) * LANES)
            v_new = (u_s[rows, hc] - res[h][:CHUNK]).astype(_BF16)
            o = res[h][CHUNK:] + jnp.dot(qk_s[i], v_new, preferred_element_type=_F32)
            for j in range(nb):
                dno_s[chunk_rows(c, j), hc] = o[cpb * j:cpb * (j + 1)]
            g_last = bd_s[last_row:last_row + 1, DN_HEADS + h:DN_HEADS + h + 1]
            state_s[h] = state_s[h] * jnp.exp(g_last) + jnp.dot(kdt_s[i], v_new,
                                                                 preferred_element_type=_F32)

    for h in range(DN_HEADS):
        hc = slice(h * LANES, (h + 1) * LANES)
        o = dno_s[:, hc]
        o = o * lax.rsqrt(jnp.mean(o * o, axis=-1, keepdims=True) + RMS_EPS) * dn_norm_ref[...]
        dno_s[:, hc] = o * zs_s[:, hc]
    y_dn = jnp.dot(dno_s[...].astype(_BF16), w_bdn_ref[...], preferred_element_type=_F32)

    y_lru = jnp.dot(ylru_s[...], w_blru_ref[...], preferred_element_type=_F32)
    merged = gates_s[:, :D_MODEL] * y_dn + gates_s[:, D_MODEL:] * y_lru
    mix = jnp.dot(merged.astype(_BF16), w_out_ref[...], preferred_element_type=_F32)
    h1 = _layer_norm(DEEPNORM_ALPHA * x + mix, ln_g_ref[...], ln_b_ref[...])
    for c in range(n_xcols):
        for j in range(nb):
            out_s[c, pl.ds(j, grp, stride=nb), :] = h1[blk(j), c * LANES:(c + 1) * LANES]
    o_ref[...] = jnp.concatenate([out_s[c] for c in range(n_xcols)], axis=1)


def _ffn_kernel(h_ref, w_gate_ref, w_up_ref, w_down_ref, ln_g_ref, ln_b_ref, o_ref):
    h = h_ref[...]
    hb = h.astype(_BF16)
    gate = jnp.dot(hb, w_gate_ref[...], preferred_element_type=_F32)
    up = jnp.dot(hb, w_up_ref[...], preferred_element_type=_F32)
    ff = jnp.dot((_silu(gate) * up).astype(_BF16), w_down_ref[...], preferred_element_type=_F32)
    o_ref[...] = _layer_norm(DEEPNORM_ALPHA * h + ff, ln_g_ref[...], ln_b_ref[...])


def _resident(shape):
    zeros = (0,) * len(shape)
    return pl.BlockSpec(shape, lambda *_: zeros, pipeline_mode=pl.Buffered(1))


def _pad_lanes(v, offset):
    return jnp.zeros((1, LANES), _F32).at[0, offset:offset + v.shape[0]].set(v.astype(_F32))


def _mixer(x, w_in, dn_conv_w, dn_a_log, dn_dt_bias, dn_norm_w, lru_conv_w, lru_conv_b,
           lru_w_a, lru_b_a, lru_w_i, lru_b_i, lru_lambda, w_branch_dn, w_branch_lru,
           b_merge_gate, w_out, ln_g, ln_b):
    bsz, seq, _ = x.shape
    tt = min(MIXER_TILE, seq)
    assert seq % tt == 0 and tt % CHUNK == 0 and CHUNK == SUBLANES * SUBLANES

    n_head_cols = W_BD_COL + 2 * DN_HEADS
    n_tail_cols = w_in.shape[1] - n_head_cols
    lane_pad = LANES - 2 * DN_HEADS
    w_in_packed = (jnp.pad(w_in[:, :n_head_cols], ((0, 0), (0, lane_pad + n_tail_cols)))
                   + jnp.pad(w_in[:, n_head_cols:], ((0, 0), (n_head_cols + lane_pad, 0)))
                   ).astype(_BF16)
    w_ai = jnp.concatenate([lru_w_a, lru_w_i], axis=-1).astype(_BF16)

    def row(v):
        return v.reshape(1, -1).astype(_F32)

    def x_cols(c):
        return pl.BlockSpec((None, tt, LANES), lambda b, t: (b, t, c))

    operands = [(x, x_cols(c)) for c in range(D_MODEL // LANES)] + [
        (w_in_packed, None),
        (dn_conv_w.astype(_F32), None),
        (_pad_lanes(dn_a_log, DN_HEADS), None), (_pad_lanes(dn_dt_bias, DN_HEADS), None),
        (row(dn_norm_w), None),
        (lru_conv_w.astype(_F32), None), (row(lru_conv_b), None), (w_ai, None),
        (row(lru_b_a), None), (row(lru_b_i), None), (row(lru_lambda), None),
        (w_branch_dn.astype(_BF16), None), (w_branch_lru.astype(_BF16), None),
        (row(b_merge_gate), None), (w_out.astype(_BF16), None), (row(ln_g), None), (row(ln_b), None),
    ]
    args = [a for a, _ in operands]
    in_specs = [spec if spec is not None else _resident(a.shape) for a, spec in operands]

    n_inst = (tt // CHUNK) * DN_HEADS
    scratch = [
        pltpu.VMEM((tt, 3 * DN_DIM), _F32),
        pltpu.VMEM((tt, LRU_WIDTH), _F32),
        pltpu.VMEM((SUBLANES, 3 * DN_DIM), _F32),
        pltpu.VMEM((SUBLANES, LRU_WIDTH), _F32),
        pltpu.VMEM((tt, LRU_WIDTH), _F32),
        pltpu.VMEM((tt, 2 * D_MODEL), _F32),
        pltpu.VMEM((tt, DN_DIM), _F32),
        pltpu.VMEM((tt, LRU_WIDTH), _BF16),
        pltpu.VMEM((tt, 3 * DN_DIM), _F32),
        pltpu.VMEM((tt, LANES), _F32),
        pltpu.VMEM((tt, DN_DIM), _F32),
        pltpu.VMEM((n_inst, 2 * CHUNK, DN_HEAD_DIM), _BF16),
        pltpu.VMEM((n_inst, CHUNK, CHUNK), _BF16),
        pltpu.VMEM((n_inst, DN_HEAD_DIM, CHUNK), _BF16),
        pltpu.VMEM((tt, DN_DIM), _F32),
        pltpu.VMEM((DN_HEADS, DN_HEAD_DIM, DN_HEAD_DIM), _F32),
        pltpu.VMEM((SUBLANES, LRU_WIDTH), _F32),
        pltpu.VMEM((D_MODEL // LANES, tt, LANES), _F32),
    ]
    return pl.pallas_call(
        _mixer_kernel,
        grid=(bsz, seq // tt),
        in_specs=in_specs,
        out_specs=pl.BlockSpec((None, tt, D_MODEL), lambda b, t: (b, t, 0)),
        out_shape=jax.ShapeDtypeStruct((bsz, seq, D_MODEL), _F32),
        scratch_shapes=scratch,
        compiler_params=pltpu.CompilerParams(
            dimension_semantics=("arbitrary", "arbitrary"), vmem_limit_bytes=VMEM_LIMIT_BYTES),
        name="hybrid_mixer_ln",
    )(*args)


def _ffn(h, w_gate, w_up, w_down, ln_g, ln_b):
    n, _ = h.shape
    tm = min(FFN_TILE, n)
    assert n % tm == 0
    args = [h, w_gate.astype(_BF16), w_up.astype(_BF16), w_down.astype(_BF16),
            ln_g.reshape(1, -1).astype(_F32), ln_b.reshape(1, -1).astype(_F32)]
    in_specs = [pl.BlockSpec((tm, D_MODEL), lambda i: (i, 0))] + [_resident(a.shape) for a in args[1:]]
    return pl.pallas_call(
        _ffn_kernel,
        grid=(n // tm,),
        in_specs=in_specs,
        out_specs=pl.BlockSpec((tm, D_MODEL), lambda i: (i, 0)),
        out_shape=jax.ShapeDtypeStruct((n, D_MODEL), _F32),
        compiler_params=pltpu.CompilerParams(
            dimension_semantics=("arbitrary",), vmem_limit_bytes=VMEM_LIMIT_BYTES),
        name="swiglu_ffn_ln",
    )(*args)


def kernel(x, w_in, dn_conv_w, dn_A_log, dn_dt_bias, dn_norm_w, lru_conv_w, lru_conv_b, lru_w_a, lru_b_a, lru_w_i, lru_b_i, lru_lambda, w_branch_dn, w_branch_lru, b_merge_gate, w_out, ln1_g, ln1_b, w_ffn_gate, w_ffn_up, w_ffn_down, ln2_g, ln2_b):
    bsz, seq, d = x.shape
    h = x
    for l in range(DEPTH):
        h = _mixer(h, w_in[l], dn_conv_w[l], dn_A_log[l], dn_dt_bias[l], dn_norm_w[l],
                   lru_conv_w[l], lru_conv_b[l], lru_w_a[l], lru_b_a[l], lru_w_i[l], lru_b_i[l],
                   lru_lambda[l], w_branch_dn[l], w_branch_lru[l], b_merge_gate[l], w_out[l],
                   ln1_g[l], ln1_b[l])
        h = _ffn(h.reshape(bsz * seq, d), w_ffn_gate[l], w_ffn_up[l], w_ffn_down[l],
                 ln2_g[l], ln2_b[l]).reshape(bsz, seq, d)
    return h
```

```python
import jax
import jax.numpy as jnp
from jax import lax
from jax.experimental import pallas as pl
from jax.experimental.pallas import tpu as pltpu

D_MODEL = 1024
DN_HEADS = 8
DN_HEAD_DIM = 128
DN_DIM = DN_HEADS * DN_HEAD_DIM
CHUNK = 64
CONV_K = 4
LRU_HEADS = 10
LRU_BLOCK = 128
LRU_WIDTH = LRU_HEADS * LRU_BLOCK
LRU_C = 8.0
DEPTH = 1
DEEPNORM_ALPHA = (2.0 * DEPTH) ** 0.25
LN_EPS = 1e-5
RMS_EPS = 1e-6

LANES = 128
SUBLANES = 8
MXU_COLS = 256
W_QKV_COL = 0
W_Z_COL = W_QKV_COL + 3 * DN_DIM
W_BD_COL = W_Z_COL + DN_DIM
W_LRU_COL = W_BD_COL + LANES
W_MG_COL = W_LRU_COL + 2 * LRU_WIDTH
MIXER_TILE = 256
FFN_TILE = 512
PACK_ROWS = 128
VMEM_LIMIT_BYTES = 56 * 1024 * 1024

_F32 = jnp.float32
_BF16 = jnp.bfloat16


def _dot_nt(a, b):
    return lax.dot_general(a.astype(_BF16), b.astype(_BF16), (((1,), (1,)), ((), ())),
                           preferred_element_type=_F32)


def _sigmoid(x):
    return 0.5 * jnp.tanh(0.5 * x) + 0.5


def _silu(x):
    h = 0.5 * x
    return h * jnp.tanh(h) + h


def _softplus(x):
    return jnp.maximum(x, 0.0) + jnp.log1p(jnp.exp(-jnp.abs(x)))


def _gelu_tanh(x):
    c = 0.7978845608028654
    return 0.5 * x * (1.0 + jnp.tanh(c * (x + 0.044715 * (x * x * x))))


def _layer_norm(x, g, b):
    mu = jnp.mean(x, axis=-1, keepdims=True)
    xc = x - mu
    var = jnp.mean(xc * xc, axis=-1, keepdims=True)
    return xc * lax.rsqrt(var + LN_EPS) * g + b


def _unit_lower_inverses(a_list, row, col):
    n = a_list[0].shape[0]
    eye = (row == col).astype(_F32)
    first = (row == col + 1) & ((col & 1) == 0)
    xs = [eye - jnp.where(first, a, 0.0) for a in a_list]
    s = 2
    while s < n:
        sh = s.bit_length() - 1
        bi = row >> sh
        bj = col >> sh
        join = (bi == bj + 1) & ((bi & 1) == 1)
        xbs = [x.astype(_BF16) for x in xs]
        ys = [jnp.dot(xb, jnp.where(join, a, 0.0).astype(_BF16), preferred_element_type=_F32)
              for xb, a in zip(xbs, a_list)]
        xs = [x - jnp.dot(y.astype(_BF16), xb, preferred_element_type=_F32)
              for x, y, xb in zip(xs, ys, xbs)]
        s *= 2
    return xs


def _mixer_kernel(*refs):
    nb = SUBLANES
    n_xcols = D_MODEL // LANES
    x_refs = refs[:n_xcols]
    (w_in_ref,
     dn_conv_ref, a_log_ref, dt_bias_ref, dn_norm_ref,
     lru_conv_w_ref, lru_conv_b_ref, w_ai_ref, b_a_ref, b_i_ref, lam_ref,
     w_bdn_ref, w_blru_ref, b_mg_ref, w_out_ref, ln_g_ref, ln_b_ref,
     o_ref,
     qkv_s, lrux_s, qkv_tail_s, lrux_tail_s, lrug_s, gates_s, zs_s, ylru_s, act_s, bd_s, u_s,
     wq_s, qk_s, kdt_s, dno_s, state_s, hl_s, out_s) = refs[n_xcols:]
    tt = x_refs[0].shape[0]
    grp = tt // nb
    cpb = CHUNK // nb
    n_chunks = tt // CHUNK
    t_idx = pl.program_id(1)

    def blk(j):
        return slice(grp * j, grp * (j + 1))

    def chunk_rows(c, j):
        return slice(grp * j + cpb * c, grp * j + cpb * (c + 1))

    def gather_chunk(ref, c, cols):
        return jnp.concatenate([ref[chunk_rows(c, j), cols] for j in range(nb)], axis=0)

    @pl.when(t_idx == 0)
    def _():
        qkv_tail_s[...] = jnp.zeros(qkv_tail_s.shape, _F32)
        lrux_tail_s[...] = jnp.zeros(lrux_tail_s.shape, _F32)
        state_s[...] = jnp.zeros(state_s.shape, _F32)
        hl_s[...] = jnp.zeros(hl_s.shape, _F32)

    x = jnp.concatenate(
        [jnp.concatenate([x_ref[pl.ds(j, grp, stride=nb), :] for x_ref in x_refs], axis=1)
         for j in range(nb)], axis=0)
    xb = x.astype(_BF16)

    row_g = lax.broadcasted_iota(jnp.int32, (grp, LANES), 0)
    row_t = lax.broadcasted_iota(jnp.int32, (tt, LANES), 0)
    first_pos = (row_t == 0) & (t_idx == 0)
    sub3 = lax.broadcasted_iota(jnp.int32, (grp // SUBLANES, SUBLANES, LANES), 1)
    neg_c_softplus = -LRU_C * _softplus(-lam_ref[...])

    def causal_conv(raw_ref, tail_ref, w_ref, cols):
        blocks = [raw_ref[blk(j), cols] for j in range(nb)]
        wrapped = {j: jnp.where(row_g == 0, tail_ref[j:j + 1, cols], pltpu.roll(blocks[j], 1, 0))
                   for j in range(nb - (CONV_K - 1), nb)}
        outs = []
        for j in range(nb):
            acc = blocks[j] * w_ref[CONV_K - 1:CONV_K, cols]
            for k in range(CONV_K - 1):
                back = CONV_K - 1 - k
                src = blocks[j - back] if j >= back else wrapped[j - back + nb]
                acc = acc + src * w_ref[k:k + 1, cols]
            outs.append(acc)
        return jnp.concatenate(outs, axis=0)

    def save_tail(raw_ref, tail_ref):
        for j in range(nb - (CONV_K - 1), nb):
            tail_ref[j:j + 1, :] = raw_ref[grp * (j + 1) - 1:grp * (j + 1), :]

    def proj_tile(base, i):
        cs = slice(i * MXU_COLS, (i + 1) * MXU_COLS)
        w = w_in_ref[:, base + i * MXU_COLS:base + (i + 1) * MXU_COLS]
        return cs, jnp.dot(xb, w, preferred_element_type=_F32)

    def qkv_task(i):
        cs, p = proj_tile(W_QKV_COL, i)
        qkv_s[:, cs] = p

    def lru_task(i):
        cs, p = proj_tile(W_LRU_COL, i)
        n_x = LRU_WIDTH // MXU_COLS
        if i < n_x:
            lrux_s[:, cs] = p
        else:
            lrug_s[:, (i - n_x) * MXU_COLS:(i - n_x + 1) * MXU_COLS] = p

    def mg_task(i):
        cs, p = proj_tile(W_MG_COL, i)
        gates_s[:, cs] = _sigmoid(p + b_mg_ref[:, cs])

    def z_task(i):
        cs, p = proj_tile(W_Z_COL, i)
        zs_s[:, cs] = _silu(p)

    def dn_conv_task(j):
        cols = slice(j * LANES, (j + 1) * LANES)
        a = _silu(causal_conv(qkv_s, qkv_tail_s, dn_conv_ref, cols))
        if j < 2 * DN_HEADS:
            a = a * lax.rsqrt(jnp.sum(a * a, axis=-1, keepdims=True) + RMS_EPS)
        if j < DN_HEADS:
            a = a * (DN_HEAD_DIM ** -0.5)
        act_s[:, cols] = a

    def lru_block_task(jh):
        cols = slice(jh * LANES, (jh + 1) * LANES)
        u = causal_conv(lrux_s, lrux_tail_s, lru_conv_w_ref, cols) + lru_conv_b_ref[:, cols]
        ai = jnp.dot(u.astype(_BF16), w_ai_ref[jh], preferred_element_type=_F32)
        r = _sigmoid(ai[:, :LANES] + b_a_ref[:, cols])
        i_gate = _sigmoid(ai[:, LANES:] + b_i_ref[:, cols])
        log_a = r * neg_c_softplus[:, cols]
        a = jnp.exp(log_a)
        one_minus_a2 = 1.0 - jnp.exp(2.0 * log_a)
        mult = jnp.where(one_minus_a2 > 0.0, one_minus_a2 * lax.rsqrt(one_minus_a2), 0.0)
        mult = jnp.where(first_pos, 1.0, mult)
        b = mult * (i_gate * u)
        h_loc, a_cum = [b[blk(0)]], [a[blk(0)]]
        for j in range(1, nb):
            h_loc.append(a[blk(j)] * h_loc[-1] + b[blk(j)])
            a_cum.append(a[blk(j)] * a_cum[-1])
        a3 = a_cum[nb - 1].reshape(grp // SUBLANES, SUBLANES, LANES)
        h3 = h_loc[nb - 1].reshape(grp // SUBLANES, SUBLANES, LANES)
        s = 1
        while s < SUBLANES:
            keep = sub3 >= s
            h3 = h3 + jnp.where(keep, a3 * pltpu.roll(h3, s, 1), 0.0)
            a3 = a3 * jnp.where(keep, pltpu.roll(a3, s, 1), 1.0)
            s *= 2
        h_prev = hl_s[0:1, cols]
        carry = h_prev
        row_end = []
        for i in range(grp // SUBLANES):
            e = h3[i] + a3[i] * carry
            carry = e[SUBLANES - 1:SUBLANES, :]
            row_end.append(e)
        hl_s[0:1, cols] = carry
        row_end = jnp.concatenate(row_end, axis=0)
        h_in = jnp.where(row_g == 0, h_prev, pltpu.roll(row_end, 1, 0))
        hcur = jnp.concatenate([h_loc[j] + a_cum[j] * h_in for j in range(nb)], axis=0)
        ylru_s[:, cols] = (hcur * _gelu_tanh(lrug_s[:, cols])).astype(_BF16)

    def tasks(fn, n):
        return [lambda i=i: fn(i) for i in range(n)]

    mxu_tasks = (tasks(qkv_task, 3 * DN_DIM // MXU_COLS) + tasks(lru_task, 2 * LRU_WIDTH // MXU_COLS)
                 + tasks(mg_task, 2 * D_MODEL // MXU_COLS) + tasks(z_task, DN_DIM // MXU_COLS))
    vpu_tasks = tasks(dn_conv_task, 3 * DN_HEADS) + tasks(lru_block_task, LRU_HEADS)
    mxu_tasks[0]()
    for i in range(max(len(mxu_tasks) - 1, len(vpu_tasks))):
        if i + 1 < len(mxu_tasks):
            mxu_tasks[i + 1]()
        if i < len(vpu_tasks):
            vpu_tasks[i]()
    save_tail(qkv_s, qkv_tail_s)
    save_tail(lrux_s, lrux_tail_s)

    bd = jnp.dot(xb, w_in_ref[:, W_BD_COL:W_BD_COL + LANES], preferred_element_type=_F32)
    beta_all = _sigmoid(bd)
    g_all = -jnp.exp(a_log_ref[...]) * _softplus(bd + dt_bias_ref[...])
    prefix = [g_all[blk(0)]]
    for j in range(1, nb):
        prefix.append(prefix[-1] + g_all[blk(j)])
    row_total = prefix[nb - 1]
    row_in_chunk = row_g & (cpb - 1)
    incl = row_total
    s = 1
    while s < cpb:
        incl = incl + jnp.where(row_in_chunk >= s, pltpu.roll(incl, s, 0), 0.0)
        s *= 2
    before_row = incl - row_total
    g_cum = jnp.concatenate([p + before_row for p in prefix], axis=0)
    lane = lax.broadcasted_iota(jnp.int32, (tt, LANES), 1)
    bd_s[...] = jnp.where(lane < DN_HEADS, beta_all, g_cum)

    cpb_shift = cpb.bit_length() - 1
    nb_shift = nb.bit_length() - 1
    r_idx = lax.broadcasted_iota(jnp.int32, (CHUNK, CHUNK), 0)
    c_idx = lax.broadcasted_iota(jnp.int32, (CHUNK, CHUNK), 1)
    row = (r_idx >> cpb_shift) + ((r_idx & (cpb - 1)) << nb_shift)
    col = (c_idx >> cpb_shift) + ((c_idx & (cpb - 1)) << nb_shift)

    a_mats, rhs_list = [], []
    for c in range(n_chunks):
        bdc = gather_chunk(bd_s, c, slice(None))
        bdt = bdc.T
        for h in range(DN_HEADS):
            i = c * DN_HEADS + h
            q = gather_chunk(act_s, c, slice(h * LANES, (h + 1) * LANES))
            k = gather_chunk(act_s, c, slice(DN_DIM + h * LANES, DN_DIM + (h + 1) * LANES))
            v = gather_chunk(act_s, c, slice(2 * DN_DIM + h * LANES, 2 * DN_DIM + (h + 1) * LANES))
            beta = bdc[:, h:h + 1]
            g_col = bdc[:, DN_HEADS + h:DN_HEADS + h + 1]
            g_row = bdt[DN_HEADS + h:DN_HEADS + h + 1, :]
            g_last = g_col[CHUNK - 1:CHUNK, :]
            decay = jnp.where(row >= col, jnp.exp(jnp.minimum(g_col - g_row, 0.0)), 0.0)
            e_g = jnp.exp(g_col)
            kb = k * beta
            kbf = k.astype(_BF16)
            kq = _dot_nt(jnp.concatenate([kb, q], axis=0), kbf)
            a_mats.append(jnp.where(row > col, kq[:CHUNK] * decay, 0.0))
            qk_s[i] = (kq[CHUNK:] * decay).astype(_BF16)
            rhs_list.append(jnp.concatenate([v * beta, kb * e_g], axis=1).astype(_BF16))
            wq_s[i, CHUNK:2 * CHUNK, :] = (q * e_g).astype(_BF16)
            kdt_s[i] = (k * jnp.exp(g_last - g_col)).T.astype(_BF16)
    t_invs = _unit_lower_inverses(a_mats, row, col)
    for c in range(n_chunks):
        rows = slice(c * CHUNK, (c + 1) * CHUNK)
        for h in range(DN_HEADS):
            i = c * DN_HEADS + h
            uw = jnp.dot(t_invs[i].astype(_BF16), rhs_list[i], preferred_element_type=_F32)
            u_s[rows, h * LANES:(h + 1) * LANES] = uw[:, :LANES]
            wq_s[i, 0:CHUNK, :] = uw[:, LANES:].astype(_BF16)

    for c in range(n_chunks):
        rows = slice(c * CHUNK, (c + 1) * CHUNK)
        last_row = grp * (nb - 1) + cpb * (c + 1) - 1
        res = []
        for h in range(DN_HEADS):
            res.append(jnp.dot(wq_s[c * DN_HEADS + h], state_s[h].astype(_BF16),
                               preferred_element_type=_F32))
        for h in range(DN_HEADS):
            i = c * DN_HEADS + h
            hc = slice(h * LANES, (h + 1) * LANES)
            v_new = (u_s[rows, hc] - res[h][:CHUNK]).astype(_BF16)
            o = res[h][CHUNK:] + jnp.dot(qk_s[i], v_new, preferred_element_type=_F32)
            for j in range(nb):
                dno_s[chunk_rows(c, j), hc] = o[cpb * j:cpb * (j + 1)]
            g_last = bd_s[last_row:last_row + 1, DN_HEADS + h:DN_HEADS + h + 1]
            state_s[h] = state_s[h] * jnp.exp(g_last) + jnp.dot(kdt_s[i], v_new,
                                                                 preferred_element_type=_F32)

    for h in range(DN_HEADS):
        hc = slice(h * LANES, (h + 1) * LANES)
        o = dno_s[:, hc]
        o = o * lax.rsqrt(jnp.mean(o * o, axis=-1, keepdims=True) + RMS_EPS) * dn_norm_ref[...]
        dno_s[:, hc] = o * zs_s[:, hc]
    y_dn = jnp.dot(dno_s[...].astype(_BF16), w_bdn_ref[...], preferred_element_type=_F32)

    y_lru = jnp.dot(ylru_s[...], w_blru_ref[...], preferred_element_type=_F32)
    merged = gates_s[:, :D_MODEL] * y_dn + gates_s[:, D_MODEL:] * y_lru
    mix = jnp.dot(merged.astype(_BF16), w_out_ref[...], preferred_element_type=_F32)
    h1 = _layer_norm(DEEPNORM_ALPHA * x + mix, ln_g_ref[...], ln_b_ref[...])
    for c in range(n_xcols):
        for j in range(nb):
            out_s[c, pl.ds(j, grp, stride=nb), :] = h1[blk(j), c * LANES:(c + 1) * LANES]
    o_ref[...] = jnp.concatenate([out_s[c] for c in range(n_xcols)], axis=1)


def _ffn_kernel(h_ref, w_gate_ref, w_up_ref, w_down_ref, ln_g_ref, ln_b_ref, o_ref):
    h = h_ref[...]
    hb = h.astype(_BF16)
    gate = jnp.dot(hb, w_gate_ref[...], preferred_element_type=_F32)
    up = jnp.dot(hb, w_up_ref[...], preferred_element_type=_F32)
    ff = jnp.dot((_silu(gate) * up).astype(_BF16), w_down_ref[...], preferred_element_type=_F32)
    o_ref[...] = _layer_norm(DEEPNORM_ALPHA * h + ff, ln_g_ref[...], ln_b_ref[...])


def _resident(shape):
    zeros = (0,) * len(shape)
    return pl.BlockSpec(shape, lambda *_: zeros, pipeline_mode=pl.Buffered(1))


def _pad_lanes(v, offset):
    return jnp.zeros((1, LANES), _F32).at[0, offset:offset + v.shape[0]].set(v.astype(_F32))


def _pack_w_in_kernel(w_ref, o_ref):
    n_head = W_BD_COL + 2 * DN_HEADS
    o_ref[:, :n_head] = w_ref[:, :n_head].astype(_BF16)
    o_ref[:, n_head:W_LRU_COL] = jnp.zeros((o_ref.shape[0], W_LRU_COL - n_head), _BF16)
    o_ref[:, W_LRU_COL:] = w_ref[:, n_head:].astype(_BF16)


def _pack_w_in(w_in):
    rows, cols = w_in.shape
    packed_cols = cols + LANES - 2 * DN_HEADS
    return pl.pallas_call(
        _pack_w_in_kernel,
        grid=(rows // PACK_ROWS,),
        in_specs=[pl.BlockSpec((PACK_ROWS, cols), lambda i: (i, 0))],
        out_specs=pl.BlockSpec((PACK_ROWS, packed_cols), lambda i: (i, 0)),
        out_shape=jax.ShapeDtypeStruct((rows, packed_cols), _BF16),
        compiler_params=pltpu.CompilerParams(dimension_semantics=("arbitrary",)),
        name="pack_w_in",
    )(w_in)


def _mixer(x, w_in, dn_conv_w, dn_a_log, dn_dt_bias, dn_norm_w, lru_conv_w, lru_conv_b,
           lru_w_a, lru_b_a, lru_w_i, lru_b_i, lru_lambda, w_branch_dn, w_branch_lru,
           b_merge_gate, w_out, ln_g, ln_b):
    bsz, seq, _ = x.shape
    tt = min(MIXER_TILE, seq)
    assert seq % tt == 0 and tt % CHUNK == 0 and CHUNK == SUBLANES * SUBLANES

    w_in_packed = _pack_w_in(w_in)
    w_ai = jnp.concatenate([lru_w_a, lru_w_i], axis=-1).astype(_BF16)

    def row(v):
        return v.reshape(1, -1).astype(_F32)

    def x_cols(c):
        return pl.BlockSpec((None, tt, LANES), lambda b, t: (b, t, c))

    operands = [(x, x_cols(c)) for c in range(D_MODEL // LANES)] + [
        (w_in_packed, None),
        (dn_conv_w.astype(_F32), None),
        (_pad_lanes(dn_a_log, DN_HEADS), None), (_pad_lanes(dn_dt_bias, DN_HEADS), None),
        (row(dn_norm_w), None),
        (lru_conv_w.astype(_F32), None), (row(lru_conv_b), None), (w_ai, None),
        (row(lru_b_a), None), (row(lru_b_i), None), (row(lru_lambda), None),
        (w_branch_dn.astype(_BF16), None), (w_branch_lru.astype(_BF16), None),
        (row(b_merge_gate), None), (w_out.astype(_BF16), None), (row(ln_g), None), (row(ln_b), None),
    ]
    args = [a for a, _ in operands]
    in_specs = [spec if spec is not None else _resident(a.shape) for a, spec in operands]

    n_inst = (tt // CHUNK) * DN_HEADS
    scratch = [
        pltpu.VMEM((tt, 3 * DN_DIM), _F32),
        pltpu.VMEM((tt, LRU_WIDTH), _F32),
        pltpu.VMEM((SUBLANES, 3 * DN_DIM), _F32),
        pltpu.VMEM((SUBLANES, LRU_WIDTH), _F32),
        pltpu.VMEM((tt, LRU_WIDTH), _F32),
        pltpu.VMEM((tt, 2 * D_MODEL), _F32),
        pltpu.VMEM((tt, DN_DIM), _F32),
        pltpu.VMEM((tt, LRU_WIDTH), _BF16),
        pltpu.VMEM((tt, 3 * DN_DIM), _F32),
        pltpu.VMEM((tt, LANES), _F32),
        pltpu.VMEM((tt, DN_DIM), _F32),
        pltpu.VMEM((n_inst, 2 * CHUNK, DN_HEAD_DIM), _BF16),
        pltpu.VMEM((n_inst, CHUNK, CHUNK), _BF16),
        pltpu.VMEM((n_inst, DN_HEAD_DIM, CHUNK), _BF16),
        pltpu.VMEM((tt, DN_DIM), _F32),
        pltpu.VMEM((DN_HEADS, DN_HEAD_DIM, DN_HEAD_DIM), _F32),
        pltpu.VMEM((SUBLANES, LRU_WIDTH), _F32),
        pltpu.VMEM((D_MODEL // LANES, tt, LANES), _F32),
    ]
    return pl.pallas_call(
        _mixer_kernel,
        grid=(bsz, seq // tt),
        in_specs=in_specs,
        out_specs=pl.BlockSpec((None, tt, D_MODEL), lambda b, t: (b, t, 0)),
        out_shape=jax.ShapeDtypeStruct((bsz, seq, D_MODEL), _F32),
        scratch_shapes=scratch,
        compiler_params=pltpu.CompilerParams(
            dimension_semantics=("arbitrary", "arbitrary"), vmem_limit_bytes=VMEM_LIMIT_BYTES),
        name="hybrid_mixer_ln",
    )(*args)


def _ffn(h, w_gate, w_up, w_down, ln_g, ln_b):
    n, _ = h.shape
    tm = min(FFN_TILE, n)
    assert n % tm == 0
    args = [h, w_gate.astype(_BF16), w_up.astype(_BF16), w_down.astype(_BF16),
            ln_g.reshape(1, -1).astype(_F32), ln_b.reshape(1, -1).astype(_F32)]
    in_specs = [pl.BlockSpec((tm, D_MODEL), lambda i: (i, 0))] + [_resident(a.shape) for a in args[1:]]
    return pl.pallas_call(
        _ffn_kernel,
        grid=(n // tm,),
        in_specs=in_specs,
        out_specs=pl.BlockSpec((tm, D_MODEL), lambda i: (i, 0)),
        out_shape=jax.ShapeDtypeStruct((n, D_MODEL), _F32),
        compiler_params=pltpu.CompilerParams(
            dimension_semantics=("arbitrary",), vmem_limit_bytes=VMEM_LIMIT_BYTES),
        name="swiglu_ffn_ln",
    )(*args)


def kernel(x, w_in, dn_conv_w, dn_A_log, dn_dt_bias, dn_norm_w, lru_conv_w, lru_conv_b, lru_w_a, lru_b_a, lru_w_i, lru_b_i, lru_lambda, w_branch_dn, w_branch_lru, b_merge_gate, w_out, ln1_g, ln1_b, w_ffn_gate, w_ffn_up, w_ffn_down, ln2_g, ln2_b):
    bsz, seq, d = x.shape
    h = x
    for l in range(DEPTH):
        h = _mixer(h, w_in[l], dn_conv_w[l], dn_A_log[l], dn_dt_bias[l], dn_norm_w[l],
                   lru_conv_w[l], lru_conv_b[l], lru_w_a[l], lru_b_a[l], lru_w_i[l], lru_b_i[l],
                   lru_lambda[l], w_branch_dn[l], w_branch_lru[l], b_merge_gate[l], w_out[l],
                   ln1_g[l], ln1_b[l])
        h = _ffn(h.reshape(bsz * seq, d), w_ffn_gate[l], w_ffn_up[l], w_ffn_down[l],
                 ln2_g[l], ln2_b[l]).reshape(bsz, seq, d)
    return h
```

```python
import jax
import jax.numpy as jnp
from jax import lax
from jax.experimental import pallas as pl
from jax.experimental.pallas import tpu as pltpu

D_MODEL = 1024
DN_HEADS = 8
DN_HEAD_DIM = 128
DN_DIM = DN_HEADS * DN_HEAD_DIM
CHUNK = 64
CONV_K = 4
LRU_HEADS = 10
LRU_BLOCK = 128
LRU_WIDTH = LRU_HEADS * LRU_BLOCK
LRU_C = 8.0
DEPTH = 1
DEEPNORM_ALPHA = (2.0 * DEPTH) ** 0.25
LN_EPS = 1e-5
RMS_EPS = 1e-6

LANES = 128
SUBLANES = 8
MXU_COLS = 256
W_QKV_COL = 0
W_Z_COL = W_QKV_COL + 3 * DN_DIM
W_BD_COL = W_Z_COL + DN_DIM
W_LRU_COL = W_BD_COL + LANES
W_MG_COL = W_LRU_COL + 2 * LRU_WIDTH
MIXER_TILE = 256
FFN_TILE = 512
PACK_ROWS = 128
VMEM_LIMIT_BYTES = 56 * 1024 * 1024

_F32 = jnp.float32
_BF16 = jnp.bfloat16


def _dot_nt(a, b):
    return lax.dot_general(a.astype(_BF16), b.astype(_BF16), (((1,), (1,)), ((), ())),
                           preferred_element_type=_F32)


def _sigmoid(x):
    return 0.5 * jnp.tanh(0.5 * x) + 0.5


def _silu(x):
    h = 0.5 * x
    return h * jnp.tanh(h) + h


def _softplus(x):
    return jnp.maximum(x, 0.0) + jnp.log1p(jnp.exp(-jnp.abs(x)))


def _gelu_tanh(x):
    c = 0.7978845608028654
    return 0.5 * x * (1.0 + jnp.tanh(c * (x + 0.044715 * (x * x * x))))


def _layer_norm(x, g, b):
    mu = jnp.mean(x, axis=-1, keepdims=True)
    xc = x - mu
    var = jnp.mean(xc * xc, axis=-1, keepdims=True)
    return xc * lax.rsqrt(var + LN_EPS) * g + b


def _unit_lower_inverses(a_list, row, col):
    n = a_list[0].shape[0]
    eye = (row == col).astype(_F32)
    first = (row == col + 1) & ((col & 1) == 0)
    xs = [eye - jnp.where(first, a, 0.0) for a in a_list]
    s = 2
    while s < n:
        sh = s.bit_length() - 1
        bi = row >> sh
        bj = col >> sh
        join = (bi == bj + 1) & ((bi & 1) == 1)
        xbs = [x.astype(_BF16) for x in xs]
        ys = [jnp.dot(xb, jnp.where(join, a, 0.0).astype(_BF16), preferred_element_type=_F32)
              for xb, a in zip(xbs, a_list)]
        xs = [x - jnp.dot(y.astype(_BF16), xb, preferred_element_type=_F32)
              for x, y, xb in zip(xs, ys, xbs)]
        s *= 2
    return xs


def _mixer_kernel(*refs):
    nb = SUBLANES
    n_xcols = D_MODEL // LANES
    x_refs = refs[:n_xcols]
    (w_in_ref,
     dn_conv_ref, a_log_ref, dt_bias_ref, dn_norm_ref,
     lru_conv_w_ref, lru_conv_b_ref, w_ai_ref, b_a_ref, b_i_ref, lam_ref,
     w_bdn_ref, w_blru_ref, b_mg_ref, w_out_ref, ln_g_ref, ln_b_ref,
     o_ref,
     qkv_s, lrux_s, qkv_tail_s, lrux_tail_s, lrug_s, gates_s, zs_s, ylru_s, act_s, bd_s, u_s,
     wq_s, qk_s, kdt_s, dno_s, state_s, hl_s, out_s) = refs[n_xcols:]
    tt = x_refs[0].shape[0]
    grp = tt // nb
    cpb = CHUNK // nb
    n_chunks = tt // CHUNK
    t_idx = pl.program_id(1)

    def blk(j):
        return slice(grp * j, grp * (j + 1))

    def chunk_rows(c, j):
        return slice(grp * j + cpb * c, grp * j + cpb * (c + 1))

    def gather_chunk(ref, c, cols):
        return jnp.concatenate([ref[chunk_rows(c, j), cols] for j in range(nb)], axis=0)

    @pl.when(t_idx == 0)
    def _():
        qkv_tail_s[...] = jnp.zeros(qkv_tail_s.shape, _F32)
        lrux_tail_s[...] = jnp.zeros(lrux_tail_s.shape, _F32)
        state_s[...] = jnp.zeros(state_s.shape, _F32)
        hl_s[...] = jnp.zeros(hl_s.shape, _F32)

    x = jnp.concatenate(
        [jnp.concatenate([x_ref[pl.ds(j, grp, stride=nb), :] for x_ref in x_refs], axis=1)
         for j in range(nb)], axis=0)
    xb = x.astype(_BF16)

    row_g = lax.broadcasted_iota(jnp.int32, (grp, LANES), 0)
    row_t = lax.broadcasted_iota(jnp.int32, (tt, LANES), 0)
    first_pos = (row_t == 0) & (t_idx == 0)
    sub3 = lax.broadcasted_iota(jnp.int32, (grp // SUBLANES, SUBLANES, LANES), 1)
    neg_c_softplus = -LRU_C * _softplus(-lam_ref[...])

    def causal_conv(raw_ref, tail_ref, w_ref, cols):
        blocks = [raw_ref[blk(j), cols] for j in range(nb)]
        wrapped = {j: jnp.where(row_g == 0, tail_ref[j:j + 1, cols], pltpu.roll(blocks[j], 1, 0))
                   for j in range(nb - (CONV_K - 1), nb)}
        outs = []
        for j in range(nb):
            acc = blocks[j] * w_ref[CONV_K - 1:CONV_K, cols]
            for k in range(CONV_K - 1):
                back = CONV_K - 1 - k
                src = blocks[j - back] if j >= back else wrapped[j - back + nb]
                acc = acc + src * w_ref[k:k + 1, cols]
            outs.append(acc)
        return jnp.concatenate(outs, axis=0)

    def save_tail(raw_ref, tail_ref):
        for j in range(nb - (CONV_K - 1), nb):
            tail_ref[j:j + 1, :] = raw_ref[grp * (j + 1) - 1:grp * (j + 1), :]

    def proj_tile(base, i):
        cs = slice(i * MXU_COLS, (i + 1) * MXU_COLS)
        w = w_in_ref[:, base + i * MXU_COLS:base + (i + 1) * MXU_COLS]
        return cs, jnp.dot(xb, w, preferred_element_type=_F32)

    def qkv_task(i):
        cs, p = proj_tile(W_QKV_COL, i)
        qkv_s[:, cs] = p

    def lru_task(i):
        cs, p = proj_tile(W_LRU_COL, i)
        n_x = LRU_WIDTH // MXU_COLS
        if i < n_x:
            lrux_s[:, cs] = p
        else:
            lrug_s[:, (i - n_x) * MXU_COLS:(i - n_x + 1) * MXU_COLS] = p

    def mg_task(i):
        cs, p = proj_tile(W_MG_COL, i)
        gates_s[:, cs] = _sigmoid(p + b_mg_ref[:, cs])

    def z_task(i):
        cs, p = proj_tile(W_Z_COL, i)
        zs_s[:, cs] = _silu(p)

    def dn_conv_task(j):
        cols = slice(j * LANES, (j + 1) * LANES)
        a = _silu(causal_conv(qkv_s, qkv_tail_s, dn_conv_ref, cols))
        if j < 2 * DN_HEADS:
            a = a * lax.rsqrt(jnp.sum(a * a, axis=-1, keepdims=True) + RMS_EPS)
        if j < DN_HEADS:
            a = a * (DN_HEAD_DIM ** -0.5)
        act_s[:, cols] = a

    def lru_block_task(jh):
        cols = slice(jh * LANES, (jh + 1) * LANES)
        u = causal_conv(lrux_s, lrux_tail_s, lru_conv_w_ref, cols) + lru_conv_b_ref[:, cols]
        ai = jnp.dot(u.astype(_BF16), w_ai_ref[jh], preferred_element_type=_F32)
        r = _sigmoid(ai[:, :LANES] + b_a_ref[:, cols])
        i_gate = _sigmoid(ai[:, LANES:] + b_i_ref[:, cols])
        log_a = r * neg_c_softplus[:, cols]
        a = jnp.exp(log_a)
        one_minus_a2 = 1.0 - jnp.exp(2.0 * log_a)
        mult = jnp.where(one_minus_a2 > 0.0, one_minus_a2 * lax.rsqrt(one_minus_a2), 0.0)
        mult = jnp.where(first_pos, 1.0, mult)
        b = mult * (i_gate * u)
        h_loc, a_cum = [b[blk(0)]], [a[blk(0)]]
        for j in range(1, nb):
            h_loc.append(a[blk(j)] * h_loc[-1] + b[blk(j)])
            a_cum.append(a[blk(j)] * a_cum[-1])
        a3 = a_cum[nb - 1].reshape(grp // SUBLANES, SUBLANES, LANES)
        h3 = h_loc[nb - 1].reshape(grp // SUBLANES, SUBLANES, LANES)
        s = 1
        while s < SUBLANES:
            keep = sub3 >= s
            h3 = h3 + jnp.where(keep, a3 * pltpu.roll(h3, s, 1), 0.0)
            a3 = a3 * jnp.where(keep, pltpu.roll(a3, s, 1), 1.0)
            s *= 2
        h_prev = hl_s[0:1, cols]
        carry = h_prev
        row_end = []
        for i in range(grp // SUBLANES):
            e = h3[i] + a3[i] * carry
            carry = e[SUBLANES - 1:SUBLANES, :]
            row_end.append(e)
        hl_s[0:1, cols] = carry
        row_end = jnp.concatenate(row_end, axis=0)
        h_in = jnp.where(row_g == 0, h_prev, pltpu.roll(row_end, 1, 0))
        hcur = jnp.concatenate([h_loc[j] + a_cum[j] * h_in for j in range(nb)], axis=0)
        ylru_s[:, cols] = (hcur * _gelu_tanh(lrug_s[:, cols])).astype(_BF16)

    def tasks(fn, n):
        return [lambda i=i: fn(i) for i in range(n)]

    mxu_tasks = (tasks(qkv_task, 3 * DN_DIM // MXU_COLS) + tasks(lru_task, 2 * LRU_WIDTH // MXU_COLS)
                 + tasks(mg_task, 2 * D_MODEL // MXU_COLS) + tasks(z_task, DN_DIM // MXU_COLS))
    vpu_tasks = tasks(dn_conv_task, 3 * DN_HEADS) + tasks(lru_block_task, LRU_HEADS)
    mxu_tasks[0]()
    for i in range(max(len(mxu_tasks) - 1, len(vpu_tasks))):
        if i + 1 < len(mxu_tasks):
            mxu_tasks[i + 1]()
        if i < len(vpu_tasks):
            vpu_tasks[i]()
    save_tail(qkv_s, qkv_tail_s)
    save_tail(lrux_s, lrux_tail_s)

    bd = jnp.dot(xb, w_in_ref[:, W_BD_COL:W_BD_COL + LANES], preferred_element_type=_F32)
    beta_all = _sigmoid(bd)
    g_all = -jnp.exp(a_log_ref[...]) * _softplus(bd + dt_bias_ref[...])
    prefix = [g_all[blk(0)]]
    for j in range(1, nb):
        prefix.append(prefix[-1] + g_all[blk(j)])
    row_total = prefix[nb - 1]
    row_in_chunk = row_g & (cpb - 1)
    incl = row_total
    s = 1
    while s < cpb:
        incl = incl + jnp.where(row_in_chunk >= s, pltpu.roll(incl, s, 0), 0.0)
        s *= 2
    before_row = incl - row_total
    g_cum = jnp.concatenate([p + before_row for p in prefix], axis=0)
    lane = lax.broadcasted_iota(jnp.int32, (tt, LANES), 1)
    bd_s[...] = jnp.where(lane < DN_HEADS, beta_all, g_cum)

    cpb_shift = cpb.bit_length() - 1
    nb_shift = nb.bit_length() - 1
    r_idx = lax.broadcasted_iota(jnp.int32, (CHUNK, CHUNK), 0)
    c_idx = lax.broadcasted_iota(jnp.int32, (CHUNK, CHUNK), 1)
    row = (r_idx >> cpb_shift) + ((r_idx & (cpb - 1)) << nb_shift)
    col = (c_idx >> cpb_shift) + ((c_idx & (cpb - 1)) << nb_shift)

    a_mats, rhs_list = [], []
    for c in range(n_chunks):
        bdc = gather_chunk(bd_s, c, slice(None))
        bdt = bdc.T
        for h in range(DN_HEADS):
            i = c * DN_HEADS + h
            q = gather_chunk(act_s, c, slice(h * LANES, (h + 1) * LANES))
            k = gather_chunk(act_s, c, slice(DN_DIM + h * LANES, DN_DIM + (h + 1) * LANES))
            v = gather_chunk(act_s, c, slice(2 * DN_DIM + h * LANES, 2 * DN_DIM + (h + 1) * LANES))
            beta = bdc[:, h:h + 1]
            g_col = bdc[:, DN_HEADS + h:DN_HEADS + h + 1]
            g_row = bdt[DN_HEADS + h:DN_HEADS + h + 1, :]
            g_last = g_col[CHUNK - 1:CHUNK, :]
            decay = jnp.where(row >= col, jnp.exp(jnp.minimum(g_col - g_row, 0.0)), 0.0)
            e_g = jnp.exp(g_col)
            kb = k * beta
            kbf = k.astype(_BF16)
            kq = _dot_nt(jnp.concatenate([kb, q], axis=0), kbf)
            a_mats.append(jnp.where(row > col, kq[:CHUNK] * decay, 0.0))
            qk_s[i] = (kq[CHUNK:] * decay).astype(_BF16)
            rhs_list.append(jnp.concatenate([v * beta, kb * e_g], axis=1).astype(_BF16))
            wq_s[i, CHUNK:2 * CHUNK, :] = (q * e_g).astype(_BF16)
            kdt_s[i] = (k * jnp.exp(g_last - g_col)).T.astype(_BF16)
    t_invs = _unit_lower_inverses(a_mats, row, col)
    for c in range(n_chunks):
        rows = slice(c * CHUNK, (c + 1) * CHUNK)
        for h in range(DN_HEADS):
            i = c * DN_HEADS + h
            uw = jnp.dot(t_invs[i].astype(_BF16), rhs_list[i], preferred_element_type=_F32)
            u_s[rows, h * LANES:(h + 1) * LANES] = uw[:, :LANES]
            wq_s[i, 0:CHUNK, :] = uw[:, LANES:].astype(_BF16)

    for c in range(n_chunks):
        rows = slice(c * CHUNK, (c + 1) * CHUNK)
        last_row = grp * (nb - 1) + cpb * (c + 1) - 1
        res = []
        for h in range(DN_HEADS):
            res.append(jnp.dot(wq_s[c * DN_HEADS + h], state_s[h].astype(_BF16),
                               preferred_element_type=_F32))
        for h in range(DN_HEADS):
            i = c * DN_HEADS + h
            hc = slice(h * LANES, (h + 1) * LANES)
            v_new = (u_s[rows, hc] - res[h][:CHUNK]).astype(_BF16)
            o = res[h][CHUNK:] + jnp.dot(qk_s[i], v_new, preferred_element_type=_F32)
            for j in range(nb):
                dno_s[chunk_rows(c, j), hc] = o[cpb * j:cpb * (j + 1)]
            g_last = bd_s[last_row:last_row + 1, DN_HEADS + h:DN_HEADS + h + 1]
            state_s[h] = state_s[h] * jnp.exp(g_last) + jnp.dot(kdt_s[i], v_new,
                                                                 preferred_element_type=_F32)

    for h in range(DN_HEADS):
        hc = slice(h * LANES, (h + 1) * LANES)
        o = dno_s[:, hc]
        o = o * lax.rsqrt(jnp.mean(o * o, axis=-1, keepdims=True) + RMS_EPS) * dn_norm_ref[...]
        dno_s[:, hc] = o * zs_s[:, hc]
    y_dn = jnp.dot(dno_s[...].astype(_BF16), w_bdn_ref[...], preferred_element_type=_F32)

    y_lru = jnp.dot(ylru_s[...], w_blru_ref[...], preferred_element_type=_F32)
    merged = gates_s[:, :D_MODEL] * y_dn + gates_s[:, D_MODEL:] * y_lru
    mix = jnp.dot(merged.astype(_BF16), w_out_ref[...], preferred_element_type=_F32)
    h1 = _layer_norm(DEEPNORM_ALPHA * x + mix, ln_g_ref[...], ln_b_ref[...])
    for c in range(n_xcols):
        for j in range(nb):
            out_s[c, pl.ds(j, grp, stride=nb), :] = h1[blk(j), c * LANES:(c + 1) * LANES]
    o_ref[...] = jnp.concatenate([out_s[c] for c in range(n_xcols)], axis=1)


def _ffn_kernel(h_ref, w_gate_ref, w_up_ref, w_down_ref, ln_g_ref, ln_b_ref, o_ref):
    h = h_ref[...]
    hb = h.astype(_BF16)
    gate = jnp.dot(hb, w_gate_ref[...], preferred_element_type=_F32)
    up = jnp.dot(hb, w_up_ref[...], preferred_element_type=_F32)
    ff = jnp.dot((_silu(gate) * up).astype(_BF16), w_down_ref[...], preferred_element_type=_F32)
    o_ref[...] = _layer_norm(DEEPNORM_ALPHA * h + ff, ln_g_ref[...], ln_b_ref[...])


def _resident(shape):
    zeros = (0,) * len(shape)
    return pl.BlockSpec(shape, lambda *_: zeros, pipeline_mode=pl.Buffered(1))


def _pad_lanes(v, offset):
    return jnp.zeros((1, LANES), _F32).at[0, offset:offset + v.shape[0]].set(v.astype(_F32))


def _pack_w_in_kernel(w_ref, o_ref):
    n_head = W_BD_COL + 2 * DN_HEADS
    o_ref[:, :n_head] = w_ref[:, :n_head].astype(_BF16)
    o_ref[:, n_head:W_LRU_COL] = jnp.zeros((o_ref.shape[0], W_LRU_COL - n_head), _BF16)
    o_ref[:, W_LRU_COL:] = w_ref[:, n_head:].astype(_BF16)


def _pack_w_in(w_in, layer):
    _, rows, cols = w_in.shape
    packed_cols = cols + LANES - 2 * DN_HEADS
    return pl.pallas_call(
        _pack_w_in_kernel,
        grid=(rows // PACK_ROWS,),
        in_specs=[pl.BlockSpec((None, PACK_ROWS, cols), lambda i: (layer, i, 0))],
        out_specs=pl.BlockSpec((PACK_ROWS, packed_cols), lambda i: (i, 0)),
        out_shape=jax.ShapeDtypeStruct((rows, packed_cols), _BF16),
        compiler_params=pltpu.CompilerParams(dimension_semantics=("arbitrary",)),
        name="pack_w_in",
    )(w_in)


def _mixer(x, w_in_packed, dn_conv_w, dn_a_log, dn_dt_bias, dn_norm_w, lru_conv_w, lru_conv_b,
           lru_w_a, lru_b_a, lru_w_i, lru_b_i, lru_lambda, w_branch_dn, w_branch_lru,
           b_merge_gate, w_out, ln_g, ln_b):
    bsz, seq, _ = x.shape
    tt = min(MIXER_TILE, seq)
    assert seq % tt == 0 and tt % CHUNK == 0 and CHUNK == SUBLANES * SUBLANES

    w_ai = jnp.concatenate([lru_w_a, lru_w_i], axis=-1).astype(_BF16)

    def row(v):
        return v.reshape(1, -1).astype(_F32)

    def x_cols(c):
        return pl.BlockSpec((None, tt, LANES), lambda b, t: (b, t, c))

    operands = [(x, x_cols(c)) for c in range(D_MODEL // LANES)] + [
        (w_in_packed, None),
        (dn_conv_w.astype(_F32), None),
        (_pad_lanes(dn_a_log, DN_HEADS), None), (_pad_lanes(dn_dt_bias, DN_HEADS), None),
        (row(dn_norm_w), None),
        (lru_conv_w.astype(_F32), None), (row(lru_conv_b), None), (w_ai, None),
        (row(lru_b_a), None), (row(lru_b_i), None), (row(lru_lambda), None),
        (w_branch_dn.astype(_BF16), None), (w_branch_lru.astype(_BF16), None),
        (row(b_merge_gate), None), (w_out.astype(_BF16), None), (row(ln_g), None), (row(ln_b), None),
    ]
    args = [a for a, _ in operands]
    in_specs = [spec if spec is not None else _resident(a.shape) for a, spec in operands]

    n_inst = (tt // CHUNK) * DN_HEADS
    scratch = [
        pltpu.VMEM((tt, 3 * DN_DIM), _F32),
        pltpu.VMEM((tt, LRU_WIDTH), _F32),
        pltpu.VMEM((SUBLANES, 3 * DN_DIM), _F32),
        pltpu.VMEM((SUBLANES, LRU_WIDTH), _F32),
        pltpu.VMEM((tt, LRU_WIDTH), _F32),
        pltpu.VMEM((tt, 2 * D_MODEL), _F32),
        pltpu.VMEM((tt, DN_DIM), _F32),
        pltpu.VMEM((tt, LRU_WIDTH), _BF16),
        pltpu.VMEM((tt, 3 * DN_DIM), _F32),
        pltpu.VMEM((tt, LANES), _F32),
        pltpu.VMEM((tt, DN_DIM), _F32),
        pltpu.VMEM((n_inst, 2 * CHUNK, DN_HEAD_DIM), _BF16),
        pltpu.VMEM((n_inst, CHUNK, CHUNK), _BF16),
        pltpu.VMEM((n_inst, DN_HEAD_DIM, CHUNK), _BF16),
        pltpu.VMEM((tt, DN_DIM), _F32),
        pltpu.VMEM((DN_HEADS, DN_HEAD_DIM, DN_HEAD_DIM), _F32),
        pltpu.VMEM((SUBLANES, LRU_WIDTH), _F32),
        pltpu.VMEM((D_MODEL // LANES, tt, LANES), _F32),
    ]
    return pl.pallas_call(
        _mixer_kernel,
        grid=(bsz, seq // tt),
        in_specs=in_specs,
        out_specs=pl.BlockSpec((None, tt, D_MODEL), lambda b, t: (b, t, 0)),
        out_shape=jax.ShapeDtypeStruct((bsz, seq, D_MODEL), _F32),
        scratch_shapes=scratch,
        compiler_params=pltpu.CompilerParams(
            dimension_semantics=("arbitrary", "arbitrary"), vmem_limit_bytes=VMEM_LIMIT_BYTES),
        name="hybrid_mixer_ln",
    )(*args)


def _ffn(h, w_gate, w_up, w_down, ln_g, ln_b):
    n, _ = h.shape
    tm = min(FFN_TILE, n)
    assert n % tm == 0
    args = [h, w_gate.astype(_BF16), w_up.astype(_BF16), w_down.astype(_BF16),
            ln_g.reshape(1, -1).astype(_F32), ln_b.reshape(1, -1).astype(_F32)]
    in_specs = [pl.BlockSpec((tm, D_MODEL), lambda i: (i, 0))] + [_resident(a.shape) for a in args[1:]]
    return pl.pallas_call(
        _ffn_kernel,
        grid=(n // tm,),
        in_specs=in_specs,
        out_specs=pl.BlockSpec((tm, D_MODEL), lambda i: (i, 0)),
        out_shape=jax.ShapeDtypeStruct((n, D_MODEL), _F32),
        compiler_params=pltpu.CompilerParams(
            dimension_semantics=("arbitrary",), vmem_limit_bytes=VMEM_LIMIT_BYTES),
        name="swiglu_ffn_ln",
    )(*args)


def kernel(x, w_in, dn_conv_w, dn_A_log, dn_dt_bias, dn_norm_w, lru_conv_w, lru_conv_b, lru_w_a, lru_b_a, lru_w_i, lru_b_i, lru_lambda, w_branch_dn, w_branch_lru, b_merge_gate, w_out, ln1_g, ln1_b, w_ffn_gate, w_ffn_up, w_ffn_down, ln2_g, ln2_b):
    bsz, seq, d = x.shape
    h = x
    for l in range(DEPTH):
        h = _mixer(h, _pack_w_in(w_in, l), dn_conv_w[l], dn_A_log[l], dn_dt_bias[l], dn_norm_w[l],
                   lru_conv_w[l], lru_conv_b[l], lru_w_a[l], lru_b_a[l], lru_w_i[l], lru_b_i[l],
                   lru_lambda[l], w_branch_dn[l], w_branch_lru[l], b_merge_gate[l], w_out[l],
                   ln1_g[l], ln1_b[l])
        h = _ffn(h.reshape(bsz * seq, d), w_ffn_gate[l], w_ffn_up[l], w_ffn_down[l],
                 ln2_g[l], ln2_b[l]).reshape(bsz, seq, d)
    return h
```

```python
import jax
import jax.numpy as jnp
from jax import lax
from jax.experimental import pallas as pl
from jax.experimental.pallas import tpu as pltpu

D_MODEL = 1024
DN_HEADS = 8
DN_HEAD_DIM = 128
DN_DIM = DN_HEADS * DN_HEAD_DIM
CHUNK = 64
CONV_K = 4
LRU_HEADS = 10
LRU_BLOCK = 128
LRU_WIDTH = LRU_HEADS * LRU_BLOCK
LRU_C = 8.0
DEPTH = 1
DEEPNORM_ALPHA = (2.0 * DEPTH) ** 0.25
LN_EPS = 1e-5
RMS_EPS = 1e-6

LANES = 128
SUBLANES = 8
MXU_COLS = 256
W_QKV_COL = 0
W_Z_COL = W_QKV_COL + 3 * DN_DIM
W_BD_COL = W_Z_COL + DN_DIM
W_LRU_COL = W_BD_COL + LANES
W_MG_COL = W_LRU_COL + 2 * LRU_WIDTH
MIXER_TILE = 256
FFN_TILE = 512
PACK_ROWS = 128
VMEM_LIMIT_BYTES = 56 * 1024 * 1024

_F32 = jnp.float32
_BF16 = jnp.bfloat16


def _dot_nt(a, b):
    return lax.dot_general(a.astype(_BF16), b.astype(_BF16), (((1,), (1,)), ((), ())),
                           preferred_element_type=_F32)


def _sigmoid(x):
    return 0.5 * jnp.tanh(0.5 * x) + 0.5


def _silu(x):
    h = 0.5 * x
    return h * jnp.tanh(h) + h


def _softplus(x):
    return jnp.maximum(x, 0.0) + jnp.log1p(jnp.exp(-jnp.abs(x)))


def _gelu_tanh(x):
    c = 0.7978845608028654
    return 0.5 * x * (1.0 + jnp.tanh(c * (x + 0.044715 * (x * x * x))))


def _layer_norm(x, g, b):
    mu = jnp.mean(x, axis=-1, keepdims=True)
    xc = x - mu
    var = jnp.mean(xc * xc, axis=-1, keepdims=True)
    return xc * lax.rsqrt(var + LN_EPS) * g + b


def _unit_lower_inverses(a_list, row, col):
    n = a_list[0].shape[0]
    eye = (row == col).astype(_F32)
    first = (row == col + 1) & ((col & 1) == 0)
    xs = [eye - jnp.where(first, a, 0.0) for a in a_list]
    s = 2
    while s < n:
        sh = s.bit_length() - 1
        bi = row >> sh
        bj = col >> sh
        join = (bi == bj + 1) & ((bi & 1) == 1)
        xbs = [x.astype(_BF16) for x in xs]
        ys = [jnp.dot(xb, jnp.where(join, a, 0.0).astype(_BF16), preferred_element_type=_F32)
              for xb, a in zip(xbs, a_list)]
        xs = [x - jnp.dot(y.astype(_BF16), xb, preferred_element_type=_F32)
              for x, y, xb in zip(xs, ys, xbs)]
        s *= 2
    return xs


def _mixer_kernel(*refs):
    nb = SUBLANES
    n_xcols = D_MODEL // LANES
    x_refs = refs[:n_xcols]
    (w_in_ref,
     dn_conv_ref, a_log_ref, dt_bias_ref, dn_norm_ref,
     lru_conv_w_ref, lru_conv_b_ref, w_ai_ref, b_a_ref, b_i_ref, lam_ref,
     w_bdn_ref, w_blru_ref, b_mg_ref, w_out_ref, ln_g_ref, ln_b_ref,
     o_ref,
     qkv_s, lrux_s, qkv_tail_s, lrux_tail_s, lrug_s, gates_s, zs_s, ylru_s, act_s, bd_s, u_s,
     wq_s, qk_s, kdt_s, dno_s, state_s, hl_s, out_s) = refs[n_xcols:]
    tt = x_refs[0].shape[0]
    grp = tt // nb
    cpb = CHUNK // nb
    n_chunks = tt // CHUNK
    t_idx = pl.program_id(1)

    def blk(j):
        return slice(grp * j, grp * (j + 1))

    def chunk_rows(c, j):
        return slice(grp * j + cpb * c, grp * j + cpb * (c + 1))

    def gather_chunk(ref, c, cols):
        return jnp.concatenate([ref[chunk_rows(c, j), cols] for j in range(nb)], axis=0)

    @pl.when(t_idx == 0)
    def _():
        qkv_tail_s[...] = jnp.zeros(qkv_tail_s.shape, _F32)
        lrux_tail_s[...] = jnp.zeros(lrux_tail_s.shape, _F32)
        state_s[...] = jnp.zeros(state_s.shape, _F32)
        hl_s[...] = jnp.zeros(hl_s.shape, _F32)

    x = jnp.concatenate(
        [jnp.concatenate([x_ref[pl.ds(j, grp, stride=nb), :] for x_ref in x_refs], axis=1)
         for j in range(nb)], axis=0)
    xb = x.astype(_BF16)

    row_g = lax.broadcasted_iota(jnp.int32, (grp, LANES), 0)
    row_t = lax.broadcasted_iota(jnp.int32, (tt, LANES), 0)
    first_pos = (row_t == 0) & (t_idx == 0)
    sub3 = lax.broadcasted_iota(jnp.int32, (grp // SUBLANES, SUBLANES, LANES), 1)
    neg_c_softplus = -LRU_C * _softplus(-lam_ref[...])

    def causal_conv(raw_ref, tail_ref, w_ref, cols):
        blocks = [raw_ref[blk(j), cols] for j in range(nb)]
        wrapped = {j: jnp.where(row_g == 0, tail_ref[j:j + 1, cols], pltpu.roll(blocks[j], 1, 0))
                   for j in range(nb - (CONV_K - 1), nb)}
        outs = []
        for j in range(nb):
            acc = blocks[j] * w_ref[CONV_K - 1:CONV_K, cols]
            for k in range(CONV_K - 1):
                back = CONV_K - 1 - k
                src = blocks[j - back] if j >= back else wrapped[j - back + nb]
                acc = acc + src * w_ref[k:k + 1, cols]
            outs.append(acc)
        return jnp.concatenate(outs, axis=0)

    def save_tail(raw_ref, tail_ref):
        for j in range(nb - (CONV_K - 1), nb):
            tail_ref[j:j + 1, :] = raw_ref[grp * (j + 1) - 1:grp * (j + 1), :]

    def proj_tile(base, i):
        cs = slice(i * MXU_COLS, (i + 1) * MXU_COLS)
        w = w_in_ref[:, base + i * MXU_COLS:base + (i + 1) * MXU_COLS]
        return cs, jnp.dot(xb, w, preferred_element_type=_F32)

    def qkv_task(i):
        cs, p = proj_tile(W_QKV_COL, i)
        qkv_s[:, cs] = p

    def lru_task(i):
        cs, p = proj_tile(W_LRU_COL, i)
        n_x = LRU_WIDTH // MXU_COLS
        if i < n_x:
            lrux_s[:, cs] = p
        else:
            lrug_s[:, (i - n_x) * MXU_COLS:(i - n_x + 1) * MXU_COLS] = p

    def mg_task(i):
        cs, p = proj_tile(W_MG_COL, i)
        gates_s[:, cs] = _sigmoid(p + b_mg_ref[:, cs])

    def z_task(i):
        cs, p = proj_tile(W_Z_COL, i)
        zs_s[:, cs] = _silu(p)

    def dn_conv_task(j):
        cols = slice(j * LANES, (j + 1) * LANES)
        a = _silu(causal_conv(qkv_s, qkv_tail_s, dn_conv_ref, cols))
        if j < 2 * DN_HEADS:
            a = a * lax.rsqrt(jnp.sum(a * a, axis=-1, keepdims=True) + RMS_EPS)
        if j < DN_HEADS:
            a = a * (DN_HEAD_DIM ** -0.5)
        act_s[:, cols] = a

    def lru_block_task(jh):
        cols = slice(jh * LANES, (jh + 1) * LANES)
        u = causal_conv(lrux_s, lrux_tail_s, lru_conv_w_ref, cols) + lru_conv_b_ref[:, cols]
        ai = jnp.dot(u.astype(_BF16), w_ai_ref[jh], preferred_element_type=_F32)
        r = _sigmoid(ai[:, :LANES] + b_a_ref[:, cols])
        i_gate = _sigmoid(ai[:, LANES:] + b_i_ref[:, cols])
        log_a = r * neg_c_softplus[:, cols]
        a = jnp.exp(log_a)
        one_minus_a2 = 1.0 - jnp.exp(2.0 * log_a)
        mult = jnp.where(one_minus_a2 > 0.0, one_minus_a2 * lax.rsqrt(one_minus_a2), 0.0)
        mult = jnp.where(first_pos, 1.0, mult)
        b = mult * (i_gate * u)
        h_loc, a_cum = [b[blk(0)]], [a[blk(0)]]
        for j in range(1, nb):
            h_loc.append(a[blk(j)] * h_loc[-1] + b[blk(j)])
            a_cum.append(a[blk(j)] * a_cum[-1])
        a3 = a_cum[nb - 1].reshape(grp // SUBLANES, SUBLANES, LANES)
        h3 = h_loc[nb - 1].reshape(grp // SUBLANES, SUBLANES, LANES)
        s = 1
        while s < SUBLANES:
            keep = sub3 >= s
            h3 = h3 + jnp.where(keep, a3 * pltpu.roll(h3, s, 1), 0.0)
            a3 = a3 * jnp.where(keep, pltpu.roll(a3, s, 1), 1.0)
            s *= 2
        h_prev = hl_s[0:1, cols]
        carry = h_prev
        row_end = []
        for i in range(grp // SUBLANES):
            e = h3[i] + a3[i] * carry
            carry = e[SUBLANES - 1:SUBLANES, :]
            row_end.append(e)
        hl_s[0:1, cols] = carry
        row_end = jnp.concatenate(row_end, axis=0)
        h_in = jnp.where(row_g == 0, h_prev, pltpu.roll(row_end, 1, 0))
        hcur = jnp.concatenate([h_loc[j] + a_cum[j] * h_in for j in range(nb)], axis=0)
        ylru_s[:, cols] = (hcur * _gelu_tanh(lrug_s[:, cols])).astype(_BF16)

    bd = jnp.dot(xb, w_in_ref[:, W_BD_COL:W_BD_COL + LANES], preferred_element_type=_F32)
    beta_all = _sigmoid(bd)
    g_all = -jnp.exp(a_log_ref[...]) * _softplus(bd + dt_bias_ref[...])
    prefix = [g_all[blk(0)]]
    for j in range(1, nb):
        prefix.append(prefix[-1] + g_all[blk(j)])
    row_total = prefix[nb - 1]
    row_in_chunk = row_g & (cpb - 1)
    incl = row_total
    s = 1
    while s < cpb:
        incl = incl + jnp.where(row_in_chunk >= s, pltpu.roll(incl, s, 0), 0.0)
        s *= 2
    before_row = incl - row_total
    g_cum = jnp.concatenate([p + before_row for p in prefix], axis=0)
    lane = lax.broadcasted_iota(jnp.int32, (tt, LANES), 1)
    bd_s[...] = jnp.where(lane < DN_HEADS, beta_all, g_cum)

    def tasks(fn, n):
        return [lambda i=i: fn(i) for i in range(n)]

    mxu_tasks = (tasks(qkv_task, 3 * DN_DIM // MXU_COLS) + tasks(lru_task, 2 * LRU_WIDTH // MXU_COLS)
                 + tasks(mg_task, 2 * D_MODEL // MXU_COLS) + tasks(z_task, DN_DIM // MXU_COLS))
    vpu_tasks = tasks(dn_conv_task, 3 * DN_HEADS) + tasks(lru_block_task, LRU_HEADS)
    mxu_tasks[0]()
    for i in range(max(len(mxu_tasks) - 1, len(vpu_tasks))):
        if i + 1 < len(mxu_tasks):
            mxu_tasks[i + 1]()
        if i < len(vpu_tasks):
            vpu_tasks[i]()
    save_tail(qkv_s, qkv_tail_s)
    save_tail(lrux_s, lrux_tail_s)

    cpb_shift = cpb.bit_length() - 1
    nb_shift = nb.bit_length() - 1
    r_idx = lax.broadcasted_iota(jnp.int32, (CHUNK, CHUNK), 0)
    c_idx = lax.broadcasted_iota(jnp.int32, (CHUNK, CHUNK), 1)
    row = (r_idx >> cpb_shift) + ((r_idx & (cpb - 1)) << nb_shift)
    col = (c_idx >> cpb_shift) + ((c_idx & (cpb - 1)) << nb_shift)

    a_mats, rhs_list = [], []
    for c in range(n_chunks):
        bdc = gather_chunk(bd_s, c, slice(None))
        bdt = bdc.T
        for h in range(DN_HEADS):
            i = c * DN_HEADS + h
            q = gather_chunk(act_s, c, slice(h * LANES, (h + 1) * LANES))
            k = gather_chunk(act_s, c, slice(DN_DIM + h * LANES, DN_DIM + (h + 1) * LANES))
            v = gather_chunk(act_s, c, slice(2 * DN_DIM + h * LANES, 2 * DN_DIM + (h + 1) * LANES))
            beta = bdc[:, h:h + 1]
            g_col = bdc[:, DN_HEADS + h:DN_HEADS + h + 1]
            g_row = bdt[DN_HEADS + h:DN_HEADS + h + 1, :]
            g_last = g_col[CHUNK - 1:CHUNK, :]
            decay = jnp.where(row >= col, jnp.exp(jnp.minimum(g_col - g_row, 0.0)), 0.0)
            e_g = jnp.exp(g_col)
            kb = k * beta
            kbf = k.astype(_BF16)
            kq = _dot_nt(jnp.concatenate([kb, q], axis=0), kbf)
            a_mats.append(jnp.where(row > col, kq[:CHUNK] * decay, 0.0))
            qk_s[i] = (kq[CHUNK:] * decay).astype(_BF16)
            rhs_list.append(jnp.concatenate([v * beta, kb * e_g], axis=1).astype(_BF16))
            wq_s[i, CHUNK:2 * CHUNK, :] = (q * e_g).astype(_BF16)
            kdt_s[i] = (k * jnp.exp(g_last - g_col)).T.astype(_BF16)
    t_invs = _unit_lower_inverses(a_mats, row, col)
    for c in range(n_chunks):
        rows = slice(c * CHUNK, (c + 1) * CHUNK)
        for h in range(DN_HEADS):
            i = c * DN_HEADS + h
            uw = jnp.dot(t_invs[i].astype(_BF16), rhs_list[i], preferred_element_type=_F32)
            u_s[rows, h * LANES:(h + 1) * LANES] = uw[:, :LANES]
            wq_s[i, 0:CHUNK, :] = uw[:, LANES:].astype(_BF16)

    for c in range(n_chunks):
        rows = slice(c * CHUNK, (c + 1) * CHUNK)
        last_row = grp * (nb - 1) + cpb * (c + 1) - 1
        res = []
        for h in range(DN_HEADS):
            res.append(jnp.dot(wq_s[c * DN_HEADS + h], state_s[h].astype(_BF16),
                               preferred_element_type=_F32))
        for h in range(DN_HEADS):
            i = c * DN_HEADS + h
            hc = slice(h * LANES, (h + 1) * LANES)
            v_new = (u_s[rows, hc] - res[h][:CHUNK]).astype(_BF16)
            o = res[h][CHUNK:] + jnp.dot(qk_s[i], v_new, preferred_element_type=_F32)
            for j in range(nb):
                dno_s[chunk_rows(c, j), hc] = o[cpb * j:cpb * (j + 1)]
            g_last = bd_s[last_row:last_row + 1, DN_HEADS + h:DN_HEADS + h + 1]
            state_s[h] = state_s[h] * jnp.exp(g_last) + jnp.dot(kdt_s[i], v_new,
                                                                 preferred_element_type=_F32)

    for h in range(DN_HEADS):
        hc = slice(h * LANES, (h + 1) * LANES)
        o = dno_s[:, hc]
        o = o * lax.rsqrt(jnp.mean(o * o, axis=-1, keepdims=True) + RMS_EPS) * dn_norm_ref[...]
        dno_s[:, hc] = o * zs_s[:, hc]
    y_dn = jnp.dot(dno_s[...].astype(_BF16), w_bdn_ref[...], preferred_element_type=_F32)

    y_lru = jnp.dot(ylru_s[...], w_blru_ref[...], preferred_element_type=_F32)
    merged = gates_s[:, :D_MODEL] * y_dn + gates_s[:, D_MODEL:] * y_lru
    mix = jnp.dot(merged.astype(_BF16), w_out_ref[...], preferred_element_type=_F32)
    h1 = _layer_norm(DEEPNORM_ALPHA * x + mix, ln_g_ref[...], ln_b_ref[...])
    for c in range(n_xcols):
        for j in range(nb):
            out_s[c, pl.ds(j, grp, stride=nb), :] = h1[blk(j), c * LANES:(c + 1) * LANES]
    o_ref[...] = jnp.concatenate([out_s[c] for c in range(n_xcols)], axis=1)


def _ffn_kernel(h_ref, w_gate_ref, w_up_ref, w_down_ref, ln_g_ref, ln_b_ref, o_ref):
    h = h_ref[...]
    hb = h.astype(_BF16)
    gate = jnp.dot(hb, w_gate_ref[...], preferred_element_type=_F32)
    up = jnp.dot(hb, w_up_ref[...], preferred_element_type=_F32)
    ff = jnp.dot((_silu(gate) * up).astype(_BF16), w_down_ref[...], preferred_element_type=_F32)
    o_ref[...] = _layer_norm(DEEPNORM_ALPHA * h + ff, ln_g_ref[...], ln_b_ref[...])


def _resident(shape):
    zeros = (0,) * len(shape)
    return pl.BlockSpec(shape, lambda *_: zeros, pipeline_mode=pl.Buffered(1))


def _pad_lanes(v, offset):
    return jnp.zeros((1, LANES), _F32).at[0, offset:offset + v.shape[0]].set(v.astype(_F32))


def _pack_w_in_kernel(w_ref, o_ref):
    n_head = W_BD_COL + 2 * DN_HEADS
    o_ref[:, :n_head] = w_ref[:, :n_head].astype(_BF16)
    o_ref[:, n_head:W_LRU_COL] = jnp.zeros((o_ref.shape[0], W_LRU_COL - n_head), _BF16)
    o_ref[:, W_LRU_COL:] = w_ref[:, n_head:].astype(_BF16)


def _pack_w_in(w_in, layer):
    _, rows, cols = w_in.shape
    packed_cols = cols + LANES - 2 * DN_HEADS
    return pl.pallas_call(
        _pack_w_in_kernel,
        grid=(rows // PACK_ROWS,),
        in_specs=[pl.BlockSpec((None, PACK_ROWS, cols), lambda i: (layer, i, 0))],
        out_specs=pl.BlockSpec((PACK_ROWS, packed_cols), lambda i: (i, 0)),
        out_shape=jax.ShapeDtypeStruct((rows, packed_cols), _BF16),
        compiler_params=pltpu.CompilerParams(dimension_semantics=("arbitrary",)),
        name="pack_w_in",
    )(w_in)


def _mixer(x, w_in_packed, dn_conv_w, dn_a_log, dn_dt_bias, dn_norm_w, lru_conv_w, lru_conv_b,
           lru_w_a, lru_b_a, lru_w_i, lru_b_i, lru_lambda, w_branch_dn, w_branch_lru,
           b_merge_gate, w_out, ln_g, ln_b):
    bsz, seq, _ = x.shape
    tt = min(MIXER_TILE, seq)
    assert seq % tt == 0 and tt % CHUNK == 0 and CHUNK == SUBLANES * SUBLANES

    w_ai = jnp.concatenate([lru_w_a, lru_w_i], axis=-1).astype(_BF16)

    def row(v):
        return v.reshape(1, -1).astype(_F32)

    def x_cols(c):
        return pl.BlockSpec((None, tt, LANES), lambda b, t: (b, t, c))

    operands = [(x, x_cols(c)) for c in range(D_MODEL // LANES)] + [
        (w_in_packed, None),
        (dn_conv_w.astype(_F32), None),
        (_pad_lanes(dn_a_log, DN_HEADS), None), (_pad_lanes(dn_dt_bias, DN_HEADS), None),
        (row(dn_norm_w), None),
        (lru_conv_w.astype(_F32), None), (row(lru_conv_b), None), (w_ai, None),
        (row(lru_b_a), None), (row(lru_b_i), None), (row(lru_lambda), None),
        (w_branch_dn.astype(_BF16), None), (w_branch_lru.astype(_BF16), None),
        (row(b_merge_gate), None), (w_out.astype(_BF16), None), (row(ln_g), None), (row(ln_b), None),
    ]
    args = [a for a, _ in operands]
    in_specs = [spec if spec is not None else _resident(a.shape) for a, spec in operands]

    n_inst = (tt // CHUNK) * DN_HEADS
    scratch = [
        pltpu.VMEM((tt, 3 * DN_DIM), _F32),
        pltpu.VMEM((tt, LRU_WIDTH), _F32),
        pltpu.VMEM((SUBLANES, 3 * DN_DIM), _F32),
        pltpu.VMEM((SUBLANES, LRU_WIDTH), _F32),
        pltpu.VMEM((tt, LRU_WIDTH), _F32),
        pltpu.VMEM((tt, 2 * D_MODEL), _F32),
        pltpu.VMEM((tt, DN_DIM), _F32),
        pltpu.VMEM((tt, LRU_WIDTH), _BF16),
        pltpu.VMEM((tt, 3 * DN_DIM), _F32),
        pltpu.VMEM((tt, LANES), _F32),
        pltpu.VMEM((tt, DN_DIM), _F32),
        pltpu.VMEM((n_inst, 2 * CHUNK, DN_HEAD_DIM), _BF16),
        pltpu.VMEM((n_inst, CHUNK, CHUNK), _BF16),
        pltpu.VMEM((n_inst, DN_HEAD_DIM, CHUNK), _BF16),
        pltpu.VMEM((tt, DN_DIM), _F32),
        pltpu.VMEM((DN_HEADS, DN_HEAD_DIM, DN_HEAD_DIM), _F32),
        pltpu.VMEM((SUBLANES, LRU_WIDTH), _F32),
        pltpu.VMEM((D_MODEL // LANES, tt, LANES), _F32),
    ]
    return pl.pallas_call(
        _mixer_kernel,
        grid=(bsz, seq // tt),
        in_specs=in_specs,
        out_specs=pl.BlockSpec((None, tt, D_MODEL), lambda b, t: (b, t, 0)),
        out_shape=jax.ShapeDtypeStruct((bsz, seq, D_MODEL), _F32),
        scratch_shapes=scratch,
        compiler_params=pltpu.CompilerParams(
            dimension_semantics=("arbitrary", "arbitrary"), vmem_limit_bytes=VMEM_LIMIT_BYTES),
        name="hybrid_mixer_ln",
    )(*args)


def _ffn(h, w_gate, w_up, w_down, ln_g, ln_b):
    n, _ = h.shape
    tm = min(FFN_TILE, n)
    assert n % tm == 0
    args = [h, w_gate.astype(_BF16), w_up.astype(_BF16), w_down.astype(_BF16),
            ln_g.reshape(1, -1).astype(_F32), ln_b.reshape(1, -1).astype(_F32)]
    in_specs = [pl.BlockSpec((tm, D_MODEL), lambda i: (i, 0))] + [_resident(a.shape) for a in args[1:]]
    return pl.pallas_call(
        _ffn_kernel,
        grid=(n // tm,),
        in_specs=in_specs,
        out_specs=pl.BlockSpec((tm, D_MODEL), lambda i: (i, 0)),
        out_shape=jax.ShapeDtypeStruct((n, D_MODEL), _F32),
        compiler_params=pltpu.CompilerParams(
            dimension_semantics=("arbitrary",), vmem_limit_bytes=VMEM_LIMIT_BYTES),
        name="swiglu_ffn_ln",
    )(*args)


def kernel(x, w_in, dn_conv_w, dn_A_log, dn_dt_bias, dn_norm_w, lru_conv_w, lru_conv_b, lru_w_a, lru_b_a, lru_w_i, lru_b_i, lru_lambda, w_branch_dn, w_branch_lru, b_merge_gate, w_out, ln1_g, ln1_b, w_ffn_gate, w_ffn_up, w_ffn_down, ln2_g, ln2_b):
    bsz, seq, d = x.shape
    h = x
    for l in range(DEPTH):
        h = _mixer(h, _pack_w_in(w_in, l), dn_conv_w[l], dn_A_log[l], dn_dt_bias[l], dn_norm_w[l],
                   lru_conv_w[l], lru_conv_b[l], lru_w_a[l], lru_b_a[l], lru_w_i[l], lru_b_i[l],
                   lru_lambda[l], w_branch_dn[l], w_branch_lru[l], b_merge_gate[l], w_out[l],
                   ln1_g[l], ln1_b[l])
        h = _ffn(h.reshape(bsz * seq, d), w_ffn_gate[l], w_ffn_up[l], w_ffn_down[l],
                 ln2_g[l], ln2_b[l]).reshape(bsz, seq, d)
    return h
```
